```python
import jax
import jax.numpy as jnp
from jax import lax
import numpy as np

D_MODEL = 1024
BATCH = 4
SEQ = 4096
DEPTH = 1
DEC_BATCH = 128
DEC_SEQ = 8
PAST_LEN = 2048
PAGE_SIZE = 128

HEAD_DIM = 64
D_MIX = D_MODEL
D_RET = D_MIX // 2
D_MOBA = D_MIX - D_RET
H_RET = D_RET // HEAD_DIM
H_MOBA = D_MOBA // HEAD_DIM
D_IN = 4 * D_RET + 3 * D_MOBA
RET_CHUNK = 128
MOBA_BLOCK = 256
MOBA_TOPK = 3
Q_BLOCK = 128
N_EXPERTS = 32
TOP_K = 4
D_FF = D_MODEL
SWIGLU_LIMIT = 7.0
SWIGLU_ALPHA = 1.702
MOE_GROUP = 128
PLE_DIM = 256
ROPE_BASE = 10000.0
EPS = 1e-6

kernel_name = 'hybrid_retention_moba_moe_decoder_step'


def rms_norm(x, w):
    xf = x.astype(jnp.float32)
    y = xf * lax.rsqrt(jnp.mean(xf * xf, axis=-1, keepdims=True) + EPS)
    return (y * w.astype(jnp.float32)).astype(x.dtype)


def rotary(x, pos):
    half = HEAD_DIM // 2
    inv = ROPE_BASE ** (-jnp.arange(half, dtype=jnp.float32) / half)
    ang = pos.astype(jnp.float32)[:, None] * inv[None, :]
    cos = jnp.cos(ang)[None, :, None, :]
    sin = jnp.sin(ang)[None, :, None, :]
    xf = x.astype(jnp.float32)
    x1, x2 = xf[..., :half], xf[..., half:]
    out = jnp.concatenate([x1 * cos - x2 * sin, x2 * cos + x1 * sin], axis=-1)
    return out.astype(x.dtype)


def retention_log_decay():
    return jnp.log1p(-jnp.exp2(-5.0 - jnp.arange(H_RET, dtype=jnp.float32)))


def project(h, w_in, q_norm_w, k_norm_w, pos):
    b, s, _ = h.shape
    z = h @ w_in
    cuts = [D_RET, 2 * D_RET, 3 * D_RET, 4 * D_RET, 4 * D_RET + D_MOBA, 4 * D_RET + 2 * D_MOBA]
    q_r, k_r, v_r, g_r, q_m, k_m, v_m = jnp.split(z, cuts, axis=-1)
    to_heads = lambda t: t.reshape(b, s, -1, HEAD_DIM)
    q_r = rotary(to_heads(q_r), pos)
    k_r = rotary(to_heads(k_r), pos) * (HEAD_DIM ** -0.5)
    v_r = to_heads(v_r)
    q_m = rms_norm(to_heads(q_m), q_norm_w)
    k_m = rms_norm(to_heads(k_m), k_norm_w)
    v_m = to_heads(v_m)
    return q_r, k_r, v_r, g_r, q_m, k_m, v_m


def retention_chunk(state, q, k, v, log_gamma):
    L = q.shape[2]
    i = jnp.arange(L, dtype=jnp.float32)
    diff = i[:, None] - i[None, :]
    lg = log_gamma[:, None, None]
    decay = jnp.where(diff >= 0, jnp.exp(lg * jnp.maximum(diff, 0.0)), 0.0)
    scores = jnp.einsum('bhid,bhjd->bhij', q, k) * decay
    cross = jnp.exp(log_gamma[:, None] * (i + 1.0))[:, :, None]
    o = jnp.einsum('bhij,bhje->bhie', scores, v) + jnp.einsum('bhid,bhde->bhie', q, state) * cross
    k_dec = k * jnp.exp(log_gamma[:, None] * (L - 1.0 - i))[:, :, None]
    new_state = jnp.exp(log_gamma * L)[:, None, None] * state + jnp.einsum('bhjd,bhje->bhde', k_dec, v)
    return o, new_state


def retention_prompt(q, k, v, log_gamma):
    b, s, h, d = q.shape
    nc = s // RET_CHUNK
    to_chunks = lambda t: t.astype(jnp.float32).reshape(b, nc, RET_CHUNK, h, d).transpose(1, 0, 3, 2, 4)
    state0 = jnp.zeros((b, h, d, d), jnp.float32)

    def step(state, qkv):
        o, state = retention_chunk(state, qkv[0], qkv[1], qkv[2], log_gamma)
        return state, o

    state, o = lax.scan(step, state0, (to_chunks(q), to_chunks(k), to_chunks(v)))
    o = o.transpose(1, 0, 3, 2, 4).reshape(b, s, h, d)
    return o, state


def retention_sample(q, k, v, state, log_gamma):
    tr = lambda t: t.astype(jnp.float32).transpose(0, 2, 1, 3)
    o, new_state = retention_chunk(state.astype(jnp.float32), tr(q), tr(k), tr(v), log_gamma)
    return o.transpose(0, 2, 1, 3), new_state


def head_group_norm(o, w, b):
    mu = jnp.mean(o, axis=-1, keepdims=True)
    var = jnp.mean(jnp.square(o - mu), axis=-1, keepdims=True)
    y = ((o - mu) * lax.rsqrt(var + EPS)).reshape(o.shape[0], o.shape[1], -1)
    return y * w.astype(jnp.float32) + b.astype(jnp.float32)


def moba_blocks(k, v):
    b, t, h, d = k.shape
    nb = -(-t // MOBA_BLOCK)
    pad = nb * MOBA_BLOCK - t
    split = lambda a: jnp.pad(a, ((0, 0), (0, pad), (0, 0), (0, 0))).reshape(b, nb, MOBA_BLOCK, h, d).transpose(0, 3, 1, 2, 4)
    kb = split(k)
    vb = split(v)
    kmean = jnp.mean(kb.astype(jnp.float32), axis=3)
    return kb, vb, kmean


def moba_attend(q, pos, kb, vb, kmean):
    h, nb = kb.shape[0], kb.shape[1]
    nq = q.shape[0]
    own = pos // MOBA_BLOCK
    qh = q.transpose(1, 0, 2).astype(jnp.float32)
    gate = jnp.einsum('hqd,hnd->hqn', qh, kmean)
    fully_past = jnp.arange(nb)[None, None, :] < own[None, :, None]
    gate = jnp.where(fully_past, gate, -jnp.inf)
    _, sel = lax.top_k(gate, min(MOBA_TOPK, nb))
    sel_valid = sel < own[None, :, None]
    blocks = jnp.concatenate([sel, jnp.broadcast_to(own[None, :, None], (h, nq, 1))], axis=-1)
    valid = jnp.concatenate([sel_valid, jnp.ones((h, nq, 1), bool)], axis=-1)
    h_idx = jnp.arange(h)[:, None, None]
    kg = kb[h_idx, blocks].astype(jnp.float32)
    vg = vb[h_idx, blocks].astype(jnp.float32)
    logits = jnp.einsum('hqd,hqnkd->hqnk', qh, kg) * (HEAD_DIM ** -0.5)
    key_pos = blocks[..., None] * MOBA_BLOCK + jnp.arange(MOBA_BLOCK)
    mask = valid[..., None] & (key_pos <= pos[None, :, None, None])
    logits = jnp.where(mask, logits, -jnp.inf)
    n_sel, blk = logits.shape[2], logits.shape[3]
    w = jax.nn.softmax(logits.reshape(h, nq, n_sel * blk), axis=-1).reshape(h, nq, n_sel, blk)
    out = jnp.einsum('hqnk,hqnkd->qhd', w, vg)
    return out.astype(q.dtype)


def moba_prompt(q, k, v):
    b, s, h, d = q.shape
    kb, vb, km = moba_blocks(k, v)
    nq = s // Q_BLOCK
    q_c = q.reshape(b * nq, Q_BLOCK, h, d)
    pos_c = jnp.tile(jnp.arange(s, dtype=jnp.int32).reshape(nq, Q_BLOCK), (b, 1))
    b_idx = jnp.repeat(jnp.arange(b, dtype=jnp.int32), nq)
    out = lax.map(lambda a: moba_attend(a[0], a[1], kb[a[2]], vb[a[2]], km[a[2]]), (q_c, pos_c, b_idx))
    return out.reshape(b, s, h, d)


def moba_sample(q, k_new, v_new, cache_k, cache_v, page_table):
    bd, sd, h, d = q.shape
    past_len = page_table.shape[1] * PAGE_SIZE
    k_past = cache_k[page_table].reshape(bd, past_len, h, d).astype(k_new.dtype)
    v_past = cache_v[page_table].reshape(bd, past_len, h, d).astype(v_new.dtype)
    k_all = jnp.concatenate([k_past, k_new], axis=1)
    v_all = jnp.concatenate([v_past, v_new], axis=1)
    kb, vb, km = moba_blocks(k_all, v_all)
    pos = past_len + jnp.arange(sd, dtype=jnp.int32)
    return lax.map(lambda a: moba_attend(a[0], pos, a[1], a[2], a[3]), (q, kb, vb, km))


def moe_ffn(h, router_w, router_b, w_gate, b_gate, w_up, b_up, w_down, b_down):
    n, d = h.shape
    logits = h.astype(jnp.float32) @ router_w.astype(jnp.float32) + router_b.astype(jnp.float32)
    top_val, top_e = lax.top_k(logits, TOP_K)
    gates = jax.nn.softmax(top_val, axis=-1)
    n_assign = n * TOP_K
    flat_e = top_e.reshape(-1)
    order = jnp.argsort(flat_e)
    sorted_e = flat_e[order]
    sorted_tok = order // TOP_K
    counts = jnp.bincount(flat_e, length=N_EXPERTS)
    starts = jnp.cumsum(counts) - counts
    padded = (counts + MOE_GROUP - 1) // MOE_GROUP * MOE_GROUP
    pad_ends = jnp.cumsum(padded)
    pad_starts = pad_ends - padded
    dest = pad_starts[sorted_e] + (jnp.arange(n_assign) - starts[sorted_e])
    n_groups = -(-n_assign // MOE_GROUP) + N_EXPERTS
    row_tok = jnp.full((n_groups * MOE_GROUP,), n, jnp.int32).at[dest].set(sorted_tok.astype(jnp.int32))
    group_e = jnp.minimum(jnp.searchsorted(pad_ends, jnp.arange(n_groups) * MOE_GROUP, side='right'), N_EXPERTS - 1)
    h_pad = jnp.concatenate([h, jnp.zeros((1, d), h.dtype)], axis=0)

    def expert_group(args):
        tok, e = args
        xb = h_pad[tok]
        g = xb @ w_gate[e] + b_gate[e]
        u = xb @ w_up[e] + b_up[e]
        g = jnp.minimum(g, SWIGLU_LIMIT)
        u = jnp.clip(u, -SWIGLU_LIMIT, SWIGLU_LIMIT)
        act = (u + 1.0) * g * jax.nn.sigmoid(SWIGLU_ALPHA * g)
        return act @ w_down[e] + b_down[e]

    rows = lax.map(expert_group, (row_tok.reshape(n_groups, MOE_GROUP), group_e))
    y_sorted = rows.reshape(-1, d)[dest]
    w_sorted = gates.reshape(-1)[order].astype(y_sorted.dtype)
    return jax.ops.segment_sum(y_sorted * w_sorted[:, None], sorted_tok, num_segments=n)


def merge_and_channel(x, o_ret, g_ret, o_moba, p, ret_gn_w, ret_gn_b, w_out, ffn_norm_w, router_w, router_b,
                      w_gate, b_gate, w_up, b_up, w_down, b_down, pl_norm_w, pl_gate_w, pl_proj_w):
    b, s, d = x.shape
    ret = head_group_norm(o_ret, ret_gn_w, ret_gn_b) * jax.nn.silu(g_ret.astype(jnp.float32))
    mix = jnp.concatenate([ret.astype(x.dtype), o_moba.reshape(b, s, D_MOBA).astype(x.dtype)], axis=-1)
    x = x + mix @ w_out
    hn = rms_norm(x, ffn_norm_w).reshape(b * s, d)
    x = x + moe_ffn(hn, router_w, router_b, w_gate, b_gate, w_up, b_up, w_down, b_down).reshape(b, s, d)
    gate = jax.nn.sigmoid((rms_norm(x, pl_norm_w) @ pl_gate_w).astype(jnp.float32))
    x = x + (gate * (p @ pl_proj_w).astype(jnp.float32)).astype(x.dtype)
    return x


def setup_inputs(seed: int = 0) -> dict:
    key = jax.random.key(seed)
    ks = jax.random.split(key, 32)
    f32 = jnp.float32
    nrm = lambda k, shape, scale: jax.random.normal(k, shape, f32) * scale
    n_pages = PAST_LEN // PAGE_SIZE
    n_used = DEC_BATCH * n_pages
    n_pool = n_used + n_used // 4
    page_table = jax.random.permutation(ks[4], n_pool)[:n_used].reshape(DEC_BATCH, n_pages).astype(jnp.int32)
    return {
        'x_prompt': nrm(ks[0], (BATCH, SEQ, D_MODEL), 1.0),
        'x_sample': nrm(ks[1], (DEC_BATCH, DEC_SEQ, D_MODEL), 1.0),
        'cache_k': nrm(ks[2], (DEPTH, n_pool, PAGE_SIZE, H_MOBA, HEAD_DIM), 1.0),
        'cache_v': nrm(ks[3], (DEPTH, n_pool, PAGE_SIZE, H_MOBA, HEAD_DIM), 1.0),
        'page_table': page_table,
        'state_ret': nrm(ks[5], (DEPTH, DEC_BATCH, H_RET, HEAD_DIM, HEAD_DIM), 1.0),
        'p_prompt': nrm(ks[6], (DEPTH, BATCH, SEQ, PLE_DIM), 1.0),
        'p_sample': nrm(ks[7], (DEPTH, DEC_BATCH, DEC_SEQ, PLE_DIM), 1.0),
        'attn_norm_w': 1.0 + nrm(ks[8], (DEPTH, D_MODEL), 0.02),
        'w_in': nrm(ks[9], (DEPTH, D_MODEL, D_IN), D_MODEL ** -0.5),
        'q_norm_w': 1.0 + nrm(ks[10], (DEPTH, HEAD_DIM), 0.02),
        'k_norm_w': 1.0 + nrm(ks[11], (DEPTH, HEAD_DIM), 0.02),
        'ret_gn_w': 1.0 + nrm(ks[12], (DEPTH, D_RET), 0.02),
        'ret_gn_b': nrm(ks[13], (DEPTH, D_RET), 0.02),
        'w_out': nrm(ks[14], (DEPTH, D_MIX, D_MODEL), D_MIX ** -0.5),
        'ffn_norm_w': 1.0 + nrm(ks[15], (DEPTH, D_MODEL), 0.02),
        'router_w': nrm(ks[16], (DEPTH, D_MODEL, N_EXPERTS), D_MODEL ** -0.5),
        'router_b': nrm(ks[17], (DEPTH, N_EXPERTS), 0.01),
        'w_gate': nrm(ks[18], (DEPTH, N_EXPERTS, D_MODEL, D_FF), D_MODEL ** -0.5),
        'b_gate': nrm(ks[19], (DEPTH, N_EXPERTS, D_FF), 0.02),
        'w_up': nrm(ks[20], (DEPTH, N_EXPERTS, D_MODEL, D_FF), D_MODEL ** -0.5),
        'b_up': nrm(ks[21], (DEPTH, N_EXPERTS, D_FF), 0.02),
        'w_down': nrm(ks[22], (DEPTH, N_EXPERTS, D_FF, D_MODEL), D_FF ** -0.5),
        'b_down': nrm(ks[23], (DEPTH, N_EXPERTS, D_MODEL), 0.02),
        'pl_norm_w': 1.0 + nrm(ks[24], (DEPTH, D_MODEL), 0.02),
        'pl_gate_w': nrm(ks[25], (DEPTH, D_MODEL, D_MODEL), D_MODEL ** -0.5),
        'pl_proj_w': nrm(ks[26], (DEPTH, PLE_DIM, D_MODEL), PLE_DIM ** -0.5),
    }


def reference(x_prompt, x_sample, cache_k, cache_v, page_table, state_ret, p_prompt, p_sample,
              attn_norm_w, w_in, q_norm_w, k_norm_w, ret_gn_w, ret_gn_b, w_out, ffn_norm_w,
              router_w, router_b, w_gate, b_gate, w_up, b_up, w_down, b_down,
              pl_norm_w, pl_gate_w, pl_proj_w):
    log_gamma = retention_log_decay()
    past_len = page_table.shape[1] * PAGE_SIZE
    pos_p = jnp.arange(x_prompt.shape[1], dtype=jnp.int32)
    pos_s = past_len + jnp.arange(x_sample.shape[1], dtype=jnp.int32)
    xp, xs = x_prompt, x_sample
    kp_l, vp_l, sp_l, ks_l, vs_l, ss_l = [], [], [], [], [], []
    for i in range(DEPTH):
        ch = (ret_gn_w[i], ret_gn_b[i], w_out[i], ffn_norm_w[i], router_w[i], router_b[i],
              w_gate[i], b_gate[i], w_up[i], b_up[i], w_down[i], b_down[i],
              pl_norm_w[i], pl_gate_w[i], pl_proj_w[i])
        hp = rms_norm(xp, attn_norm_w[i])
        q_r, k_r, v_r, g_r, q_m, k_m, v_m = project(hp, w_in[i], q_norm_w[i], k_norm_w[i], pos_p)
        o_ret, st_p = retention_prompt(q_r, k_r, v_r, log_gamma)
        o_moba = moba_prompt(q_m, k_m, v_m)
        xp = merge_and_channel(xp, o_ret, g_r, o_moba, p_prompt[i], *ch)
        kp_l.append(k_m)
        vp_l.append(v_m)
        sp_l.append(st_p)
        hs = rms_norm(xs, attn_norm_w[i])
        q_r, k_r, v_r, g_r, q_m, k_m, v_m = project(hs, w_in[i], q_norm_w[i], k_norm_w[i], pos_s)
        o_ret, st_s = retention_sample(q_r, k_r, v_r, state_ret[i], log_gamma)
        o_moba = moba_sample(q_m, k_m, v_m, cache_k[i], cache_v[i], page_table)
        xs = merge_and_channel(xs, o_ret, g_r, o_moba, p_sample[i], *ch)
        ks_l.append(k_m)
        vs_l.append(v_m)
        ss_l.append(st_s)
    return (xp, xs, jnp.stack(kp_l), jnp.stack(vp_l), jnp.stack(sp_l), jnp.stack(ks_l), jnp.stack(vs_l), jnp.stack(ss_l))
```

```python
import functools

import jax
import jax.numpy as jnp
import numpy as np
from jax import lax
from jax.experimental import pallas as pl
from jax.experimental.pallas import tpu as pltpu

F32 = jnp.float32
BF16 = jnp.bfloat16

HEAD_DIM = 64
N_HEADS = 8
D_GROUP = N_HEADS * HEAD_DIM
RET_CHUNK = 128
MOBA_BLOCK = 256
MOBA_TOPK = 3
Q_BLOCK = 128
PAGE_SIZE = 128
N_EXPERTS = 32
TOP_K = 4
SWIGLU_LIMIT = 7.0
SWIGLU_ALPHA = 1.702
ROPE_BASE = 10000.0
EPS = 1e-6
NEG_INF = float("-inf")

LANES = 128
PROJ_TILE = 256
TOKEN_TILE = 256
MOE_ROWS = 128
VMEM_LIMIT = 56 * 1024 * 1024


def _cparams(sem, vmem=VMEM_LIMIT):
    return pltpu.CompilerParams(dimension_semantics=sem, vmem_limit_bytes=vmem)


def _dot(a, b):
    return jnp.dot(a, b, preferred_element_type=F32)


def _dot_nt(a, b):
    return lax.dot_general(a, b, (((1,), (1,)), ((), ())), preferred_element_type=F32)


def _split_dot(x, a):
    hi = x.astype(BF16)
    lo = (x - hi.astype(F32)).astype(BF16)
    return _dot(hi, a) + _dot(lo, a)


def _head_blockdiag(scale):
    i = np.arange(D_GROUP) // HEAD_DIM
    return jnp.asarray((i[:, None] == i[None, :]).astype(np.float32) * scale, BF16)


def _proj_kernel(x_ref, nw_ref, w_ref, qn_ref, kn_ref, cos_ref, sin_ref, bd_ref, *out_refs, prompt):
    x = x_ref[...]
    h = x * lax.rsqrt(jnp.mean(x * x, axis=-1, keepdims=True) + EPS) * nw_ref[...]
    hb = h.astype(BF16)
    cos = cos_ref[...]
    sin = sin_ref[...]
    lane = lax.broadcasted_iota(jnp.int32, (1, D_GROUP), 1)
    first_half = (lane % HEAD_DIM) < (HEAD_DIM // 2)

    def zcol(c):
        return _dot(hb, w_ref[:, c * D_GROUP:(c + 1) * D_GROUP])

    def rot(z):
        partner = jnp.where(first_half, pltpu.roll(z, D_GROUP - HEAD_DIM // 2, 1), pltpu.roll(z, HEAD_DIM // 2, 1))
        return z * cos + partner * sin

    def qk_norm(z, w):
        ms = _split_dot(z * z, bd_ref[...])
        return z * lax.rsqrt(ms + EPS) * w

    q_r = rot(zcol(0))
    k_r = rot(zcol(1)) * (HEAD_DIM ** -0.5)
    v_r = zcol(2)
    g_r = zcol(3)
    q_m = qk_norm(zcol(4), qn_ref[...])
    k_m = qk_norm(zcol(5), kn_ref[...])
    v_m = zcol(6)
    if prompt:
        qr_ref, kr_ref, vr_ref, gr_ref, km_ref, vm_ref, qt_ref, kb_ref, vt_ref, kmean_ref = out_refs
        qt_ref[...] = q_m.T
        kb_ref[...] = k_m.astype(BF16)
        vt_ref[...] = v_m.T.astype(BF16)
        kmean_ref[...] = jnp.mean(k_m, axis=0, keepdims=True)
    else:
        qr_ref, kr_ref, vr_ref, gr_ref, qm_ref, km_ref, vm_ref = out_refs
        qm_ref[...] = q_m
    qr_ref[...] = q_r
    kr_ref[...] = k_r
    vr_ref[...] = v_r
    gr_ref[...] = g_r
    km_ref[...] = k_m
    vm_ref[...] = v_m


def _rope_tables(pos):
    half = HEAD_DIM // 2
    inv = ROPE_BASE ** (-jnp.arange(half, dtype=F32) / half)
    ang = pos.astype(F32)[:, None] * inv[None, :]
    cos = jnp.cos(ang)
    sin = jnp.sin(ang)
    cos_h = jnp.concatenate([cos, cos], axis=-1)
    sin_h = jnp.concatenate([-sin, sin], axis=-1)
    return jnp.tile(cos_h, (1, N_HEADS)), jnp.tile(sin_h, (1, N_HEADS))


def _proj(x2d, pos_table, attn_norm_w, w_in_b, q_norm_w, k_norm_w, *, prompt, seq_len=None):
    n, d = x2d.shape
    tm = PROJ_TILE
    cos, sin = _rope_tables(pos_table)
    p = pos_table.shape[0]
    if p < tm:
        cos = jnp.tile(cos, (tm // p, 1))
        sin = jnp.tile(sin, (tm // p, 1))
        p = tm
    tabs = p // tm
    nt = n // tm
    row = lambda i: (i, 0)
    full = lambda i: (0, 0)
    in_specs = [
        pl.BlockSpec((tm, d), row),
        pl.BlockSpec((1, d), full),
        pl.BlockSpec(w_in_b.shape, full),
        pl.BlockSpec((1, D_GROUP), full),
        pl.BlockSpec((1, D_GROUP), full),
        pl.BlockSpec((tm, D_GROUP), lambda i: (i % tabs, 0)),
        pl.BlockSpec((tm, D_GROUP), lambda i: (i % tabs, 0)),
        pl.BlockSpec((D_GROUP, D_GROUP), full),
    ]
    tok = jax.ShapeDtypeStruct((n, D_GROUP), F32)
    tok_spec = pl.BlockSpec((tm, D_GROUP), row)
    if prompt:
        nb = n // seq_len
        tps = seq_len // tm
        t_spec = pl.BlockSpec((None, D_GROUP, tm), lambda i: (i // tps, 0, i % tps))
        out_shape = [tok] * 6 + [
            jax.ShapeDtypeStruct((nb, D_GROUP, seq_len), F32),
            jax.ShapeDtypeStruct((n, D_GROUP), BF16),
            jax.ShapeDtypeStruct((nb, D_GROUP, seq_len), BF16),
            jax.ShapeDtypeStruct((nt, 1, D_GROUP), F32),
        ]
        out_specs = [tok_spec] * 6 + [t_spec, tok_spec, t_spec, pl.BlockSpec((None, 1, D_GROUP), lambda i: (i, 0, 0))]
    else:
        out_shape = [tok] * 7
        out_specs = [tok_spec] * 7
    return pl.pallas_call(
        functools.partial(_proj_kernel, prompt=prompt),
        grid=(nt,),
        in_specs=in_specs,
        out_specs=out_specs,
        out_shape=out_shape,
        compiler_params=_cparams(("parallel",)),
    )(x2d, attn_norm_w.reshape(1, d), w_in_b, jnp.tile(q_norm_w, N_HEADS).reshape(1, D_GROUP),
      jnp.tile(k_norm_w, N_HEADS).reshape(1, D_GROUP), cos, sin, _head_blockdiag(1.0 / HEAD_DIM))


N_PAIRS = N_HEADS // 2


def _retention_log_decay():
    return jnp.log1p(-jnp.exp2(-5.0 - jnp.arange(N_HEADS, dtype=F32)))


def _retention_tables(L):
    lg = _retention_log_decay()
    i = jnp.arange(L, dtype=F32)
    diff = i[:, None] - i[None, :]
    dec = jnp.where(diff >= 0, jnp.exp(lg[:, None, None] * jnp.maximum(diff, 0.0)), 0.0)
    cross = jnp.repeat(jnp.exp(lg[:, None] * (i + 1.0)).T, HEAD_DIM, axis=1)
    kdec = jnp.repeat(jnp.exp(lg[:, None] * (L - 1.0 - i)).T, HEAD_DIM, axis=1)
    sg = jnp.exp(lg * L)
    blk = np.arange(LANES) // HEAD_DIM
    bd = jnp.asarray((blk[:, None] == blk[None, :]).astype(np.float32))
    sg_lane = jnp.repeat(sg.reshape(N_PAIRS, 2), HEAD_DIM, axis=1)
    sgam = sg_lane[:, :, None] * bd[None]
    return dec, cross, kdec, sgam, bd


def _retention_kernel(q_ref, k_ref, v_ref, s0_ref, dec_ref, cross_ref, kdec_ref, sgam_ref, bd_ref,
                      o_ref, st_ref, s_sc, *, L, seqs, use_init):
    c = pl.program_id(1)

    @pl.when(c == 0)
    def _():
        if use_init:
            s_sc[...] = s0_ref[...]
        else:
            s_sc[...] = jnp.zeros_like(s_sc)

    lane = lax.broadcasted_iota(jnp.int32, (1, LANES), 1)
    head0 = lane < HEAD_DIM
    bd = bd_ref[...]
    for s in range(seqs):
        rows = slice(s * L, (s + 1) * L)
        for p in range(N_PAIRS):
            cols = slice(p * LANES, (p + 1) * LANES)
            q2 = q_ref[rows, cols]
            k2 = k_ref[rows, cols]
            v2 = v_ref[rows, cols]
            q2b = q2.astype(BF16)
            v2b = v2.astype(BF16)
            state = s_sc[s, p]
            o = _dot(q2b, state.astype(BF16)) * cross_ref[:, cols]
            for j in range(2):
                mj = head0 if j == 0 else jnp.logical_not(head0)
                kj = jnp.where(mj, k2, 0.0).astype(BF16)
                sc = _dot_nt(q2b, kj) * dec_ref[2 * p + j]
                o = o + jnp.where(mj, _dot(sc.astype(BF16), v2b), 0.0)
            o_ref[rows, cols] = o
            kd = (k2 * kdec_ref[:, cols]).astype(BF16)
            upd = lax.dot_general(kd, v2b, (((0,), (0,)), ((), ())), preferred_element_type=F32)
            s_sc[s, p] = sgam_ref[p] * state + bd * upd

    @pl.when(c == pl.num_programs(1) - 1)
    def _():
        st_ref[...] = s_sc[...]


def _pair_states(state):
    n = state.shape[0]
    s = state.reshape(n, N_PAIRS, 2, HEAD_DIM, HEAD_DIM)
    z = jnp.zeros_like(s[:, :, 0])
    top = jnp.concatenate([s[:, :, 0], z], axis=-1)
    bot = jnp.concatenate([z, s[:, :, 1]], axis=-1)
    return jnp.concatenate([top, bot], axis=-2)


def _unpair_states(sp):
    n = sp.shape[0]
    a = sp[:, :, :HEAD_DIM, :HEAD_DIM]
    b = sp[:, :, HEAD_DIM:, HEAD_DIM:]
    return jnp.stack([a, b], axis=2).reshape(n, N_HEADS, HEAD_DIM, HEAD_DIM)


def _retention(q, k, v, *, n_seq, L, n_chunks, seqs_per_step, init_state=None):
    n = q.shape[0]
    assert n == n_seq * n_chunks * L and n_seq % seqs_per_step == 0
    assert seqs_per_step == 1 or n_chunks == 1
    dec, cross, kdec, sgam, bd = _retention_tables(L)
    use_init = init_state is not None
    s0 = _pair_states(init_state.astype(F32)) if use_init else jnp.zeros((n_seq, N_PAIRS, LANES, LANES), F32)
    sb = seqs_per_step
    rows = sb * L
    tok_spec = pl.BlockSpec((rows, D_GROUP), lambda b, c: (b * n_chunks + c, 0))
    st_spec = pl.BlockSpec((sb, N_PAIRS, LANES, LANES), lambda b, c: (b, 0, 0, 0))
    const = lambda shape: pl.BlockSpec(shape, lambda b, c: (0,) * len(shape))
    o, st = pl.pallas_call(
        functools.partial(_retention_kernel, L=L, seqs=sb, use_init=use_init),
        grid=(n_seq // sb, n_chunks),
        in_specs=[tok_spec, tok_spec, tok_spec, st_spec, const(dec.shape), const(cross.shape), const(kdec.shape),
                  const(sgam.shape), const(bd.shape)],
        out_specs=[tok_spec, st_spec],
        out_shape=[jax.ShapeDtypeStruct((n, D_GROUP), F32), jax.ShapeDtypeStruct(s0.shape, F32)],
        scratch_shapes=[pltpu.VMEM((sb, N_PAIRS, LANES, LANES), F32)],
        compiler_params=_cparams(("parallel", "arbitrary")),
    )(q, k, v, s0, dec, cross, kdec, sgam, bd)
    return o, _unpair_states(st)


def _top_blocks_t(gate_t, n_valid, rows):
    blk = lax.broadcasted_iota(jnp.int32, gate_t.shape, 0)
    g = jnp.where(blk < n_valid, gate_t, NEG_INF)
    sel = jnp.zeros(gate_t.shape, F32)
    for _ in range(MOBA_TOPK):
        m = jnp.max(g, axis=0, keepdims=True)
        idx = jnp.min(jnp.where(g == m, blk, rows), axis=0, keepdims=True)
        hit = (blk == idx) & (blk < n_valid)
        sel = jnp.where(hit, 1.0, sel)
        g = jnp.where(blk == idx, NEG_INF, g)
    return sel


def _moba_prompt_kernel(qt_ref, kb_ref, vt_ref, kmean_ref, o_ref, sel_sc, out_sc, *, n_blocks):
    qi = pl.program_id(1)
    own = qi // (MOBA_BLOCK // Q_BLOCK)
    sub = lax.broadcasted_iota(jnp.int32, (LANES, 1), 0)
    key_row = lax.broadcasted_iota(jnp.int32, (MOBA_BLOCK, Q_BLOCK), 0)
    q_col = lax.broadcasted_iota(jnp.int32, (MOBA_BLOCK, Q_BLOCK), 1)
    causal = (own * MOBA_BLOCK + key_row) <= (qi * Q_BLOCK + q_col)

    for h in range(N_HEADS):
        p, j = divmod(h, 2)
        cols = slice(p * LANES, (p + 1) * LANES)
        hrows = slice(h * HEAD_DIM, (h + 1) * HEAD_DIM)
        q2t = qt_ref[p * LANES:(p + 1) * LANES, :]
        in_head = (sub >= j * HEAD_DIM) & (sub < (j + 1) * HEAD_DIM)
        qht = jnp.where(in_head, q2t, 0.0)
        gate_t = jnp.dot(kmean_ref[:, cols], qht, preferred_element_type=F32, precision=lax.Precision.HIGHEST)
        sel_sc[...] = _top_blocks_t(gate_t, own, n_blocks)
        qhb = (qht * (HEAD_DIM ** -0.5)).astype(BF16)

        def block_logits(n):
            kblk = kb_ref[pl.ds(pl.multiple_of(n * MOBA_BLOCK, MOBA_BLOCK), MOBA_BLOCK), cols]
            return _dot(kblk, qhb)

        def block_values(n):
            return vt_ref[hrows, pl.ds(pl.multiple_of(n * MOBA_BLOCK, MOBA_BLOCK), MOBA_BLOCK)]

        lg = jnp.where(causal, block_logits(own), NEG_INF)
        m0 = jnp.max(lg, axis=0, keepdims=True)
        p0 = jnp.exp(lg - m0)
        l0 = jnp.sum(p0, axis=0, keepdims=True)
        acc0 = _dot(block_values(own), p0.astype(BF16))

        def body(n, carry):
            m, l, acc = carry
            keep = sel_sc[pl.ds(n, 1), :] > 0.0
            lg = jnp.where(keep, block_logits(n), NEG_INF)
            m_new = jnp.maximum(m, jnp.max(lg, axis=0, keepdims=True))
            alpha = jnp.exp(m - m_new)
            pr = jnp.exp(lg - m_new)
            l = alpha * l + jnp.sum(pr, axis=0, keepdims=True)
            acc = alpha * acc + _dot(block_values(n), pr.astype(BF16))
            return m_new, l, acc

        _, l, acc = lax.fori_loop(0, own, body, (m0, l0, acc0))
        out_sc[hrows, :] = acc / l
    o_ref[...] = out_sc[...].T


def _moba_prompt(qt, kb, vt, kmean, *, n_seq, seq_len):
    nq = seq_len // Q_BLOCK
    nblk = seq_len // MOBA_BLOCK
    return pl.pallas_call(
        functools.partial(_moba_prompt_kernel, n_blocks=nblk),
        grid=(n_seq, nq),
        in_specs=[
            pl.BlockSpec((None, D_GROUP, Q_BLOCK), lambda b, i: (b, 0, i)),
            pl.BlockSpec((seq_len, D_GROUP), lambda b, i: (b, 0)),
            pl.BlockSpec((None, D_GROUP, seq_len), lambda b, i: (b, 0, 0)),
            pl.BlockSpec((None, nblk, D_GROUP), lambda b, i: (b, 0, 0)),
        ],
        out_specs=pl.BlockSpec((Q_BLOCK, D_GROUP), lambda b, i: (b * nq + i, 0)),
        out_shape=jax.ShapeDtypeStruct((n_seq * seq_len, D_GROUP), F32),
        scratch_shapes=[pltpu.VMEM((nblk, Q_BLOCK), F32), pltpu.VMEM((D_GROUP, Q_BLOCK), F32)],
        compiler_params=_cparams(("parallel", "arbitrary")),
    )(qt, kb, vt, kmean.reshape(n_seq, nblk, D_GROUP))


def _moba_sample_kernel(pt_ref, q_ref, kn_ref, vn_ref, kc_ref, vc_ref, o_ref,
                        m_sc, l_sc, acc_sc, ksum_sc, *, n_pages, n_new):
    pg = pl.program_id(1)
    ppb = MOBA_BLOCK // PAGE_SIZE
    n_past = n_pages // ppb
    blk = pg // ppb
    nrow = n_new * N_HEADS
    prow = PAGE_SIZE * N_HEADS

    q = q_ref[...]
    qb = (q * (HEAD_DIM ** -0.5)).astype(BF16)
    kpage = kc_ref[...].reshape(prow, HEAD_DIM)
    vpage = vc_ref[...].reshape(prow, HEAD_DIM)
    r_head = lax.broadcasted_iota(jnp.int32, (nrow, prow), 0) % N_HEADS
    c_head = lax.broadcasted_iota(jnp.int32, (nrow, prow), 1) % N_HEADS
    lg = jnp.where(r_head == c_head, _dot_nt(qb, kpage.astype(BF16)), NEG_INF)
    m_pg = jnp.max(lg, axis=-1, keepdims=True)
    p_pg = jnp.exp(lg - m_pg)
    l_pg = jnp.sum(p_pg, axis=-1, keepdims=True)
    a_pg = _dot(p_pg.astype(BF16), vpage.astype(BF16))
    ks_pg = jnp.sum(kc_ref[...], axis=0)

    @pl.when(pg % ppb == 0)
    def _():
        m_sc[blk] = m_pg
        l_sc[blk] = l_pg
        acc_sc[blk] = a_pg
        ksum_sc[blk] = ks_pg

    @pl.when(pg % ppb != 0)
    def _():
        m_old = m_sc[blk]
        m_new = jnp.maximum(m_old, m_pg)
        a_old = jnp.exp(m_old - m_new)
        a_cur = jnp.exp(m_pg - m_new)
        m_sc[blk] = m_new
        l_sc[blk] = a_old * l_sc[blk] + a_cur * l_pg
        acc_sc[blk] = a_old * acc_sc[blk] + a_cur * a_pg
        ksum_sc[blk] = ksum_sc[blk] + ks_pg

    @pl.when(pg == n_pages - 1)
    def _():
        rr = lax.broadcasted_iota(jnp.int32, (nrow, nrow), 0)
        cc = lax.broadcasted_iota(jnp.int32, (nrow, nrow), 1)
        ok = ((rr % N_HEADS) == (cc % N_HEADS)) & ((cc // N_HEADS) <= (rr // N_HEADS))
        lg_o = jnp.where(ok, _dot_nt(qb, kn_ref[...].astype(BF16)), NEG_INF)
        m = jnp.max(lg_o, axis=-1, keepdims=True)
        p_o = jnp.exp(lg_o - m)
        l = jnp.sum(p_o, axis=-1, keepdims=True)
        acc = _dot(p_o.astype(BF16), vn_ref[...].astype(BF16))
        hrow = lax.broadcasted_iota(jnp.int32, (nrow, N_HEADS), 0) % N_HEADS
        hcol = lax.broadcasted_iota(jnp.int32, (nrow, N_HEADS), 1)
        gates = []
        for n in range(n_past):
            kmean = ksum_sc[n] * (1.0 / MOBA_BLOCK)
            g = lax.dot_general(q, kmean, (((1,), (1,)), ((), ())), preferred_element_type=F32,
                                precision=lax.Precision.HIGHEST)
            gates.append(jnp.sum(jnp.where(hrow == hcol, g, 0.0), axis=-1, keepdims=True))
        taken = [jnp.zeros((nrow, 1), F32) for _ in range(n_past)]
        for _ in range(min(MOBA_TOPK, n_past)):
            best = gates[0]
            for n in range(1, n_past):
                best = jnp.maximum(best, gates[n])
            found = jnp.zeros((nrow, 1), F32)
            for n in range(n_past):
                hit = (gates[n] == best) & (found == 0.0)
                taken[n] = jnp.where(hit, 1.0, taken[n])
                found = jnp.where(hit, 1.0, found)
                gates[n] = jnp.where(hit, NEG_INF, gates[n])
        for n in range(n_past):
            m_n = jnp.where(taken[n] > 0.0, m_sc[n], NEG_INF)
            m_new = jnp.maximum(m, m_n)
            a_old = jnp.exp(m - m_new)
            a_n = jnp.exp(m_n - m_new)
            l = a_old * l + a_n * l_sc[n]
            acc = a_old * acc + a_n * acc_sc[n]
            m = m_new
        o_ref[...] = acc / l


def _moba_sample(q_m, k_new, v_new, cache_k, cache_v, page_table):
    n_seq, n_pages = page_table.shape
    n_new = q_m.shape[0] // n_seq
    nrow = n_new * N_HEADS
    n_past = n_pages // (MOBA_BLOCK // PAGE_SIZE)
    rows = lambda a: a.reshape(n_seq, nrow, HEAD_DIM)
    new_spec = pl.BlockSpec((None, nrow, HEAD_DIM), lambda b, g, pt: (b, 0, 0))
    page_spec = pl.BlockSpec((None, PAGE_SIZE, N_HEADS, HEAD_DIM), lambda b, g, pt: (pt[b * n_pages + g], 0, 0, 0))
    out = pl.pallas_call(
        functools.partial(_moba_sample_kernel, n_pages=n_pages, n_new=n_new),
        grid_spec=pltpu.PrefetchScalarGridSpec(
            num_scalar_prefetch=1,
            grid=(n_seq, n_pages),
            in_specs=[new_spec, new_spec, new_spec, page_spec, page_spec],
            out_specs=new_spec,
            scratch_shapes=[pltpu.VMEM((n_past, nrow, 1), F32), pltpu.VMEM((n_past, nrow, 1), F32),
                            pltpu.VMEM((n_past, nrow, HEAD_DIM), F32), pltpu.VMEM((n_past, N_HEADS, HEAD_DIM), F32)],
        ),
        out_shape=jax.ShapeDtypeStruct((n_seq, nrow, HEAD_DIM), F32),
        compiler_params=_cparams(("parallel", "arbitrary")),
    )(page_table.reshape(-1), rows(q_m), rows(k_new), rows(v_new), cache_k, cache_v)
    return out.reshape(n_seq * n_new, D_GROUP)


def _merge_kernel(x_ref, oret_ref, g_ref, omoba_ref, gnw_ref, gnb_ref, wout_ref, fnw_ref, rw_ref, rb_ref, avg_ref,
                  x1_ref, hn_ref, te_ref, gt_ref):
    avg = avg_ref[...]
    o = oret_ref[...]
    mu = _split_dot(o, avg)
    d = o - mu
    var = _split_dot(d * d, avg)
    y = d * lax.rsqrt(var + EPS) * gnw_ref[...] + gnb_ref[...]
    g = g_ref[...]
    ret = y * (g * jax.nn.sigmoid(g))
    x1 = x_ref[...] + _dot(ret.astype(BF16), wout_ref[:D_GROUP, :]) + _dot(omoba_ref[...].astype(BF16), wout_ref[D_GROUP:, :])
    x1_ref[...] = x1
    hn = x1 * lax.rsqrt(jnp.mean(x1 * x1, axis=-1, keepdims=True) + EPS) * fnw_ref[...]
    hn_ref[...] = hn
    logits = jnp.dot(hn, rw_ref[...], preferred_element_type=F32, precision=lax.Precision.HIGHEST) + rb_ref[...]
    tm = logits.shape[0]
    eidx = lax.broadcasted_iota(jnp.int32, logits.shape, 1)
    lane = lax.broadcasted_iota(jnp.int32, (tm, LANES), 1)
    vals = jnp.zeros((tm, LANES), F32)
    idxs = jnp.zeros((tm, LANES), jnp.int32)
    top = None
    for t in range(TOP_K):
        m = jnp.max(logits, axis=-1, keepdims=True)
        idx = jnp.min(jnp.where(logits == m, eidx, N_EXPERTS), axis=-1, keepdims=True)
        if t == 0:
            top = m
        vals = jnp.where(lane == t, jnp.exp(m - top), vals)
        idxs = jnp.where(lane == t, idx, idxs)
        logits = jnp.where(eidx == idx, NEG_INF, logits)
    te_ref[...] = idxs
    gt_ref[...] = vals / jnp.sum(vals, axis=-1, keepdims=True)


def _merge(x, o_ret, g_r, o_moba, ret_gn_w, ret_gn_b, w_out_b, ffn_norm_w, router_w, router_b):
    n, d = x.shape
    tm = TOKEN_TILE
    row = lambda i: (i, 0)
    full = lambda i: (0, 0)
    return pl.pallas_call(
        _merge_kernel,
        grid=(n // tm,),
        in_specs=[
            pl.BlockSpec((tm, d), row), pl.BlockSpec((tm, D_GROUP), row), pl.BlockSpec((tm, D_GROUP), row),
            pl.BlockSpec((tm, D_GROUP), row), pl.BlockSpec((1, D_GROUP), full), pl.BlockSpec((1, D_GROUP), full),
            pl.BlockSpec(w_out_b.shape, full), pl.BlockSpec((1, d), full), pl.BlockSpec(router_w.shape, full),
            pl.BlockSpec((1, N_EXPERTS), full), pl.BlockSpec((D_GROUP, D_GROUP), full),
        ],
        out_specs=[pl.BlockSpec((tm, d), row), pl.BlockSpec((tm, d), row),
                   pl.BlockSpec((tm, LANES), row), pl.BlockSpec((tm, LANES), row)],
        out_shape=[jax.ShapeDtypeStruct((n, d), F32), jax.ShapeDtypeStruct((n, d), F32),
                   jax.ShapeDtypeStruct((n, LANES), jnp.int32), jax.ShapeDtypeStruct((n, LANES), F32)],
        compiler_params=_cparams(("parallel",)),
    )(x, o_ret, g_r, o_moba, ret_gn_w.reshape(1, -1), ret_gn_b.reshape(1, -1), w_out_b, ffn_norm_w.reshape(1, d),
      router_w, router_b.reshape(1, -1), _head_blockdiag(1.0 / HEAD_DIM))


def _experts_kernel(ge_ref, slot_ref, hn_hbm, wg_ref, bg_ref, wu_ref, bu_ref, wd_ref, bd_ref, y_hbm,
                    xbuf, ybuf, wgb, wub, wdb, gsem, ssem):
    g = pl.program_id(0)
    base = g * MOE_ROWS

    def row_in(r):
        slot = slot_ref[base + r]
        tok = jnp.maximum(slot, 0) // TOP_K
        return pltpu.make_async_copy(hn_hbm.at[pl.ds(tok, 1), :], xbuf.at[pl.ds(r, 1), :], gsem)

    def row_out(r):
        slot = jnp.maximum(slot_ref[base + r], 0)
        return pltpu.make_async_copy(ybuf.at[pl.ds(r, 1), :], y_hbm.at[pl.ds(slot, 1), :], ssem)

    def start_in(r, c):
        row_in(r).start()
        return c

    lax.fori_loop(0, MOE_ROWS, start_in, 0)

    e_now = ge_ref[g]
    e_prev = ge_ref[jnp.maximum(g - 1, 0)]

    @pl.when((g == 0) | (e_now != e_prev))
    def _():
        wgb[...] = wg_ref[...].astype(BF16)
        wub[...] = wu_ref[...].astype(BF16)
        wdb[...] = wd_ref[...].astype(BF16)

    def wait_in(r, c):
        row_in(r).wait()
        return c

    lax.fori_loop(0, MOE_ROWS, wait_in, 0)

    xb = xbuf[...].astype(BF16)
    gate = jnp.minimum(_dot(xb, wgb[...]) + bg_ref[...], SWIGLU_LIMIT)
    up = jnp.clip(_dot(xb, wub[...]) + bu_ref[...], -SWIGLU_LIMIT, SWIGLU_LIMIT)
    act = (up + 1.0) * gate * jax.nn.sigmoid(SWIGLU_ALPHA * gate)
    ybuf[...] = _dot(act.astype(BF16), wdb[...]) + bd_ref[...]

    def start_out(r, c):
        @pl.when(slot_ref[base + r] >= 0)
        def _():
            row_out(r).start()
        return c

    lax.fori_loop(0, MOE_ROWS, start_out, 0)

    def wait_out(r, c):
        @pl.when(slot_ref[base + r] >= 0)
        def _():
            row_out(r).wait()
        return c

    lax.fori_loop(0, MOE_ROWS, wait_out, 0)


def _experts(hn, group_e, row_slot, w_gate, b_gate, w_up, b_up, w_down, b_down):
    n, d = hn.shape
    n_groups = group_e.shape[0]
    d_ff = w_gate.shape[-1]
    wspec = lambda shape: pl.BlockSpec((None,) + shape, lambda g, ge, rs: (ge[g], 0, 0))
    return pl.pallas_call(
        _experts_kernel,
        grid_spec=pltpu.PrefetchScalarGridSpec(
            num_scalar_prefetch=2,
            grid=(n_groups,),
            in_specs=[pl.BlockSpec(memory_space=pl.ANY), wspec((d, d_ff)), wspec((1, d_ff)), wspec((d, d_ff)),
                      wspec((1, d_ff)), wspec((d_ff, d)), wspec((1, d))],
            out_specs=pl.BlockSpec(memory_space=pl.ANY),
            scratch_shapes=[pltpu.VMEM((MOE_ROWS, d), F32), pltpu.VMEM((MOE_ROWS, d), F32),
                            pltpu.VMEM((d, d_ff), BF16), pltpu.VMEM((d, d_ff), BF16), pltpu.VMEM((d_ff, d), BF16),
                            pltpu.SemaphoreType.DMA(()), pltpu.SemaphoreType.DMA(())],
        ),
        out_shape=jax.ShapeDtypeStruct((n * TOP_K, d), F32),
        compiler_params=_cparams(("arbitrary",)),
    )(group_e, row_slot, hn, w_gate, b_gate.reshape(N_EXPERTS, 1, d_ff), w_up, b_up.reshape(N_EXPERTS, 1, d_ff),
      w_down, b_down.reshape(N_EXPERTS, 1, d))


def _route(top_e):
    n = top_e.shape[0]
    n_assign = n * TOP_K
    flat_e = top_e.reshape(-1)
    order = jnp.argsort(flat_e)
    sorted_e = flat_e[order]
    counts = jnp.bincount(flat_e, length=N_EXPERTS)
    starts = jnp.cumsum(counts) - counts
    padded = (counts + MOE_ROWS - 1) // MOE_ROWS * MOE_ROWS
    pad_ends = jnp.cumsum(padded)
    pad_starts = pad_ends - padded
    dest = pad_starts[sorted_e] + (jnp.arange(n_assign) - starts[sorted_e])
    n_groups = -(-n_assign // MOE_ROWS) + N_EXPERTS
    row_slot = jnp.full((n_groups * MOE_ROWS,), -1, jnp.int32).at[dest].set(order.astype(jnp.int32))
    group_e = jnp.minimum(jnp.searchsorted(pad_ends, jnp.arange(n_groups) * MOE_ROWS, side='right'), N_EXPERTS - 1)
    return group_e.astype(jnp.int32), row_slot


def _final_kernel(x1_ref, y_ref, gt_ref, p_ref, nw_ref, wg_ref, wp_ref, o_ref):
    d = x1_ref.shape[-1]
    gt = gt_ref[...]
    x2 = x1_ref[...]
    for k in range(TOP_K):
        x2 = x2 + y_ref[:, k * d:(k + 1) * d] * gt[:, k:k + 1]
    hn = x2 * lax.rsqrt(jnp.mean(x2 * x2, axis=-1, keepdims=True) + EPS) * nw_ref[...]
    gate = jax.nn.sigmoid(_dot(hn.astype(BF16), wg_ref[...]))
    o_ref[...] = x2 + gate * _dot(p_ref[...].astype(BF16), wp_ref[...])


def _final(x1, y4, gates, p, pl_norm_w, pl_gate_b, pl_proj_b):
    n, d = x1.shape
    tm = TOKEN_TILE
    row = lambda i: (i, 0)
    full = lambda i: (0, 0)
    return pl.pallas_call(
        _final_kernel,
        grid=(n // tm,),
        in_specs=[pl.BlockSpec((tm, d), row), pl.BlockSpec((tm, TOP_K * d), row), pl.BlockSpec((tm, LANES), row),
                  pl.BlockSpec((tm, p.shape[-1]), row), pl.BlockSpec((1, d), full), pl.BlockSpec(pl_gate_b.shape, full),
                  pl.BlockSpec(pl_proj_b.shape, full)],
        out_specs=pl.BlockSpec((tm, d), row),
        out_shape=jax.ShapeDtypeStruct((n, d), F32),
        compiler_params=_cparams(("parallel",)),
    )(x1, y4, gates, p, pl_norm_w.reshape(1, d), pl_gate_b, pl_proj_b)


def kernel(x_prompt, x_sample, cache_k, cache_v, page_table, state_ret, p_prompt, p_sample, attn_norm_w, w_in, q_norm_w, k_norm_w, ret_gn_w, ret_gn_b, w_out, ffn_norm_w, router_w, router_b, w_gate, b_gate, w_up, b_up, w_down, b_down, pl_norm_w, pl_gate_w, pl_proj_w):
    depth = w_in.shape[0]
    b, s, d = x_prompt.shape
    bd, sd, _ = x_sample.shape
    n_p = b * s
    n_s = bd * sd
    past_len = page_table.shape[1] * PAGE_SIZE
    pos_p = jnp.arange(s, dtype=jnp.int32)
    pos_s = past_len + jnp.arange(sd, dtype=jnp.int32)
    xp = x_prompt.reshape(n_p, d)
    xs = x_sample.reshape(n_s, d)
    kp_l, vp_l, sp_l, ks_l, vs_l, ss_l = [], [], [], [], [], []
    for i in range(depth):
        w_in_b = w_in[i].astype(BF16)
        qr, kr, vr, gr_p, km_p, vm_p, qt, kb, vt, kmean = _proj(
            xp, pos_p, attn_norm_w[i], w_in_b, q_norm_w[i], k_norm_w[i], prompt=True, seq_len=s)
        oret_p, st_p = _retention(qr, kr, vr, n_seq=b, L=RET_CHUNK, n_chunks=s // RET_CHUNK, seqs_per_step=1)
        omoba_p = _moba_prompt(qt, kb, vt, kmean, n_seq=b, seq_len=s)
        qr, kr, vr, gr_s, qm_s, km_s, vm_s = _proj(
            xs, pos_s, attn_norm_w[i], w_in_b, q_norm_w[i], k_norm_w[i], prompt=False)
        oret_s, st_s = _retention(qr, kr, vr, n_seq=bd, L=sd, n_chunks=1, seqs_per_step=8, init_state=state_ret[i])
        omoba_s = _moba_sample(qm_s, km_s, vm_s, cache_k[i], cache_v[i], page_table)
        cat = lambda a_, b_: jnp.concatenate([a_, b_], axis=0)
        x1, hn, top_e, gates = _merge(
            cat(xp, xs), cat(oret_p, oret_s), cat(gr_p, gr_s), cat(omoba_p, omoba_s), ret_gn_w[i], ret_gn_b[i],
            w_out[i].astype(BF16), ffn_norm_w[i], router_w[i], router_b[i])
        group_e, row_slot = _route(top_e[:, :TOP_K])
        y = _experts(hn, group_e, row_slot, w_gate[i], b_gate[i], w_up[i], b_up[i], w_down[i], b_down[i])
        x_all = _final(x1, y.reshape(n_p + n_s, TOP_K * d), gates,
                       cat(p_prompt[i].reshape(n_p, -1), p_sample[i].reshape(n_s, -1)),
                       pl_norm_w[i], pl_gate_w[i].astype(BF16), pl_proj_w[i].astype(BF16))
        xp, xs = x_all[:n_p], x_all[n_p:]
        kp_l.append(km_p.reshape(b, s, N_HEADS, HEAD_DIM))
        vp_l.append(vm_p.reshape(b, s, N_HEADS, HEAD_DIM))
        sp_l.append(st_p)
        ks_l.append(km_s.reshape(bd, sd, N_HEADS, HEAD_DIM))
        vs_l.append(vm_s.reshape(bd, sd, N_HEADS, HEAD_DIM))
        ss_l.append(st_s)
    return (xp.reshape(b, s, d), xs.reshape(bd, sd, d), jnp.stack(kp_l), jnp.stack(vp_l), jnp.stack(sp_l),
            jnp.stack(ks_l), jnp.stack(vs_l), jnp.stack(ss_l))
```

```python
import functools

import jax
import jax.numpy as jnp
import numpy as np
from jax import lax
from jax.experimental import pallas as pl
from jax.experimental.pallas import tpu as pltpu

F32 = jnp.float32
BF16 = jnp.bfloat16

HEAD_DIM = 64
N_HEADS = 8
D_GROUP = N_HEADS * HEAD_DIM
RET_CHUNK = 128
MOBA_BLOCK = 256
MOBA_TOPK = 3
Q_BLOCK = 128
PAGE_SIZE = 128
N_EXPERTS = 32
TOP_K = 4
SWIGLU_LIMIT = 7.0
SWIGLU_ALPHA = 1.702
ROPE_BASE = 10000.0
EPS = 1e-6
NEG_INF = float("-inf")

LANES = 128
PROJ_TILE = 256
TOKEN_TILE = 256
MOE_ROWS = 128
VMEM_LIMIT = 56 * 1024 * 1024


def _cparams(sem, vmem=VMEM_LIMIT):
    return pltpu.CompilerParams(dimension_semantics=sem, vmem_limit_bytes=vmem)


def _dot(a, b):
    return jnp.dot(a, b, preferred_element_type=F32)


def _dot_nt(a, b):
    return lax.dot_general(a, b, (((1,), (1,)), ((), ())), preferred_element_type=F32)


def _split_dot(x, a):
    hi = x.astype(BF16)
    lo = (x - hi.astype(F32)).astype(BF16)
    return _dot(hi, a) + _dot(lo, a)


def _head_blockdiag(scale):
    i = np.arange(D_GROUP) // HEAD_DIM
    return jnp.asarray((i[:, None] == i[None, :]).astype(np.float32) * scale, BF16)


def _proj_kernel(x_ref, nw_ref, w_ref, qn_ref, kn_ref, cos_ref, sin_ref, bd_ref, *out_refs, prompt):
    x = x_ref[...]
    h = x * lax.rsqrt(jnp.mean(x * x, axis=-1, keepdims=True) + EPS) * nw_ref[...]
    hb = h.astype(BF16)
    cos = cos_ref[...]
    sin = sin_ref[...]
    lane = lax.broadcasted_iota(jnp.int32, (1, D_GROUP), 1)
    first_half = (lane % HEAD_DIM) < (HEAD_DIM // 2)

    def zcol(c):
        return _dot(hb, w_ref[:, c * D_GROUP:(c + 1) * D_GROUP])

    def rot(z):
        partner = jnp.where(first_half, pltpu.roll(z, D_GROUP - HEAD_DIM // 2, 1), pltpu.roll(z, HEAD_DIM // 2, 1))
        return z * cos + partner * sin

    def qk_norm(z, w):
        ms = _split_dot(z * z, bd_ref[...])
        return z * lax.rsqrt(ms + EPS) * w

    q_r = rot(zcol(0))
    k_r = rot(zcol(1)) * (HEAD_DIM ** -0.5)
    v_r = zcol(2)
    g_r = zcol(3)
    q_m = qk_norm(zcol(4), qn_ref[...])
    k_m = qk_norm(zcol(5), kn_ref[...])
    v_m = zcol(6)
    if prompt:
        qr_ref, kr_ref, vr_ref, gr_ref, qt_ref, kt_ref, vt_ref, kb_ref, vtb_ref, kmean_ref = out_refs
        qt_ref[...] = q_m.T
        kt_ref[...] = k_m.T
        v_t = v_m.T
        vt_ref[...] = v_t
        vtb_ref[...] = v_t.astype(BF16)
        kb_ref[...] = k_m.astype(BF16)
        kmean_ref[...] = jnp.mean(k_m, axis=0, keepdims=True)
    else:
        qr_ref, kr_ref, vr_ref, gr_ref, qm_ref, km_ref, vm_ref = out_refs
        qm_ref[...] = q_m
        km_ref[...] = k_m
        vm_ref[...] = v_m
    qr_ref[...] = q_r
    kr_ref[...] = k_r
    vr_ref[...] = v_r
    gr_ref[...] = g_r


def _rope_tables(pos):
    half = HEAD_DIM // 2
    inv = ROPE_BASE ** (-jnp.arange(half, dtype=F32) / half)
    ang = pos.astype(F32)[:, None] * inv[None, :]
    cos = jnp.cos(ang)
    sin = jnp.sin(ang)
    cos_h = jnp.concatenate([cos, cos], axis=-1)
    sin_h = jnp.concatenate([-sin, sin], axis=-1)
    return jnp.tile(cos_h, (1, N_HEADS)), jnp.tile(sin_h, (1, N_HEADS))


def _proj(x2d, pos_table, attn_norm_w, w_in_b, q_norm_w, k_norm_w, *, prompt, seq_len=None):
    n, d = x2d.shape
    tm = PROJ_TILE
    cos, sin = _rope_tables(pos_table)
    p = pos_table.shape[0]
    if p < tm:
        cos = jnp.tile(cos, (tm // p, 1))
        sin = jnp.tile(sin, (tm // p, 1))
        p = tm
    tabs = p // tm
    nt = n // tm
    row = lambda i: (i, 0)
    full = lambda i: (0, 0)
    in_specs = [
        pl.BlockSpec((tm, d), row),
        pl.BlockSpec((1, d), full),
        pl.BlockSpec(w_in_b.shape, full),
        pl.BlockSpec((1, D_GROUP), full),
        pl.BlockSpec((1, D_GROUP), full),
        pl.BlockSpec((tm, D_GROUP), lambda i: (i % tabs, 0)),
        pl.BlockSpec((tm, D_GROUP), lambda i: (i % tabs, 0)),
        pl.BlockSpec((D_GROUP, D_GROUP), full),
    ]
    tok = jax.ShapeDtypeStruct((n, D_GROUP), F32)
    tok_spec = pl.BlockSpec((tm, D_GROUP), row)
    if prompt:
        nb = n // seq_len
        tps = seq_len // tm
        t_spec = pl.BlockSpec((None, D_GROUP, tm), lambda i: (i // tps, 0, i % tps))
        t_f32 = jax.ShapeDtypeStruct((nb, D_GROUP, seq_len), F32)
        out_shape = [tok] * 4 + [t_f32] * 3 + [
            jax.ShapeDtypeStruct((n, D_GROUP), BF16),
            jax.ShapeDtypeStruct((nb, D_GROUP, seq_len), BF16),
            jax.ShapeDtypeStruct((nt, 1, D_GROUP), F32),
        ]
        out_specs = [tok_spec] * 4 + [t_spec] * 3 + [tok_spec, t_spec,
                                                     pl.BlockSpec((None, 1, D_GROUP), lambda i: (i, 0, 0))]
    else:
        out_shape = [tok] * 7
        out_specs = [tok_spec] * 7
    return pl.pallas_call(
        functools.partial(_proj_kernel, prompt=prompt),
        grid=(nt,),
        in_specs=in_specs,
        out_specs=out_specs,
        out_shape=out_shape,
        compiler_params=_cparams(("parallel",)),
        name="proj_prompt" if prompt else "proj_sample",
    )(x2d, attn_norm_w.reshape(1, d), w_in_b, jnp.tile(q_norm_w, N_HEADS).reshape(1, D_GROUP),
      jnp.tile(k_norm_w, N_HEADS).reshape(1, D_GROUP), cos, sin, _head_blockdiag(1.0 / HEAD_DIM))


N_PAIRS = N_HEADS // 2


def _retention_log_decay():
    return jnp.log1p(-jnp.exp2(-5.0 - jnp.arange(N_HEADS, dtype=F32)))


def _retention_tables(L):
    lg = _retention_log_decay()
    i = jnp.arange(L, dtype=F32)
    diff = i[:, None] - i[None, :]
    dec = jnp.where(diff >= 0, jnp.exp(lg[:, None, None] * jnp.maximum(diff, 0.0)), 0.0)
    cross = jnp.repeat(jnp.exp(lg[:, None] * (i + 1.0)).T, HEAD_DIM, axis=1)
    kdec = jnp.repeat(jnp.exp(lg[:, None] * (L - 1.0 - i)).T, HEAD_DIM, axis=1)
    sg = jnp.exp(lg * L)
    blk = np.arange(LANES) // HEAD_DIM
    bd = jnp.asarray((blk[:, None] == blk[None, :]).astype(np.float32))
    sg_lane = jnp.repeat(sg.reshape(N_PAIRS, 2), HEAD_DIM, axis=1)
    sgam = sg_lane[:, :, None] * bd[None]
    return dec, cross, kdec, sgam, bd


def _retention_kernel(q_ref, k_ref, v_ref, s0_ref, dec_ref, cross_ref, kdec_ref, sgam_ref, bd_ref,
                      o_ref, st_ref, s_sc, *, L, seqs, use_init):
    c = pl.program_id(1)

    @pl.when(c == 0)
    def _():
        if use_init:
            s_sc[...] = s0_ref[...]
        else:
            s_sc[...] = jnp.zeros_like(s_sc)

    lane = lax.broadcasted_iota(jnp.int32, (1, LANES), 1)
    head0 = lane < HEAD_DIM
    bd = bd_ref[...]
    for s in range(seqs):
        rows = slice(s * L, (s + 1) * L)
        for p in range(N_PAIRS):
            cols = slice(p * LANES, (p + 1) * LANES)
            q2 = q_ref[rows, cols]
            k2 = k_ref[rows, cols]
            v2 = v_ref[rows, cols]
            q2b = q2.astype(BF16)
            v2b = v2.astype(BF16)
            state = s_sc[s, p]
            o = _dot(q2b, state.astype(BF16)) * cross_ref[:, cols]
            for j in range(2):
                mj = head0 if j == 0 else jnp.logical_not(head0)
                kj = jnp.where(mj, k2, 0.0).astype(BF16)
                sc = _dot_nt(q2b, kj) * dec_ref[2 * p + j]
                o = o + jnp.where(mj, _dot(sc.astype(BF16), v2b), 0.0)
            o_ref[rows, cols] = o
            kd = (k2 * kdec_ref[:, cols]).astype(BF16)
            upd = lax.dot_general(kd, v2b, (((0,), (0,)), ((), ())), preferred_element_type=F32)
            s_sc[s, p] = sgam_ref[p] * state + bd * upd

    @pl.when(c == pl.num_programs(1) - 1)
    def _():
        st_ref[...] = s_sc[...]


def _pair_states(state):
    n = state.shape[0]
    s = state.reshape(n, N_PAIRS, 2, HEAD_DIM, HEAD_DIM)
    z = jnp.zeros_like(s[:, :, 0])
    top = jnp.concatenate([s[:, :, 0], z], axis=-1)
    bot = jnp.concatenate([z, s[:, :, 1]], axis=-1)
    return jnp.concatenate([top, bot], axis=-2)


def _unpair_states(sp):
    n = sp.shape[0]
    a = sp[:, :, :HEAD_DIM, :HEAD_DIM]
    b = sp[:, :, HEAD_DIM:, HEAD_DIM:]
    return jnp.stack([a, b], axis=2).reshape(n, N_HEADS, HEAD_DIM, HEAD_DIM)


def _retention(q, k, v, *, n_seq, L, n_chunks, seqs_per_step, init_state=None):
    n = q.shape[0]
    assert n == n_seq * n_chunks * L and n_seq % seqs_per_step == 0
    assert seqs_per_step == 1 or n_chunks == 1
    dec, cross, kdec, sgam, bd = _retention_tables(L)
    use_init = init_state is not None
    s0 = _pair_states(init_state.astype(F32)) if use_init else jnp.zeros((n_seq, N_PAIRS, LANES, LANES), F32)
    sb = seqs_per_step
    rows = sb * L
    tok_spec = pl.BlockSpec((rows, D_GROUP), lambda b, c: (b * n_chunks + c, 0))
    st_spec = pl.BlockSpec((sb, N_PAIRS, LANES, LANES), lambda b, c: (b, 0, 0, 0))
    const = lambda shape: pl.BlockSpec(shape, lambda b, c: (0,) * len(shape))
    o, st = pl.pallas_call(
        functools.partial(_retention_kernel, L=L, seqs=sb, use_init=use_init),
        grid=(n_seq // sb, n_chunks),
        in_specs=[tok_spec, tok_spec, tok_spec, st_spec, const(dec.shape), const(cross.shape), const(kdec.shape),
                  const(sgam.shape), const(bd.shape)],
        out_specs=[tok_spec, st_spec],
        out_shape=[jax.ShapeDtypeStruct((n, D_GROUP), F32), jax.ShapeDtypeStruct(s0.shape, F32)],
        scratch_shapes=[pltpu.VMEM((sb, N_PAIRS, LANES, LANES), F32)],
        compiler_params=_cparams(("parallel", "arbitrary")),
        name="retention_sample" if use_init else "retention_prompt",
    )(q, k, v, s0, dec, cross, kdec, sgam, bd)
    return o, _unpair_states(st)


def _top_blocks_t(gate_t, n_valid, rows):
    blk = lax.broadcasted_iota(jnp.int32, gate_t.shape, 0)
    g = jnp.where(blk < n_valid, gate_t, NEG_INF)
    sel = jnp.zeros(gate_t.shape, F32)
    for _ in range(MOBA_TOPK):
        m = jnp.max(g, axis=0, keepdims=True)
        idx = jnp.min(jnp.where(g == m, blk, rows), axis=0, keepdims=True)
        hit = (blk == idx) & (blk < n_valid)
        sel = jnp.where(hit, 1.0, sel)
        g = jnp.where(blk == idx, NEG_INF, g)
    return sel


def _moba_prompt_kernel(qt_ref, kb_ref, vt_ref, kmean_ref, o_ref, sel_sc, out_sc, *, n_blocks):
    qi = pl.program_id(1)
    own = qi // (MOBA_BLOCK // Q_BLOCK)
    sub = lax.broadcasted_iota(jnp.int32, (LANES, 1), 0)
    key_row = lax.broadcasted_iota(jnp.int32, (MOBA_BLOCK, Q_BLOCK), 0)
    q_col = lax.broadcasted_iota(jnp.int32, (MOBA_BLOCK, Q_BLOCK), 1)
    causal = (own * MOBA_BLOCK + key_row) <= (qi * Q_BLOCK + q_col)

    for h in range(N_HEADS):
        p, j = divmod(h, 2)
        cols = slice(p * LANES, (p + 1) * LANES)
        hrows = slice(h * HEAD_DIM, (h + 1) * HEAD_DIM)
        q2t = qt_ref[p * LANES:(p + 1) * LANES, :]
        in_head = (sub >= j * HEAD_DIM) & (sub < (j + 1) * HEAD_DIM)
        qht = jnp.where(in_head, q2t, 0.0)
        gate_t = jnp.dot(kmean_ref[:, cols], qht, preferred_element_type=F32, precision=lax.Precision.HIGHEST)
        sel_sc[...] = _top_blocks_t(gate_t, own, n_blocks)
        qhb = (qht * (HEAD_DIM ** -0.5)).astype(BF16)

        def block_logits(n):
            kblk = kb_ref[pl.ds(pl.multiple_of(n * MOBA_BLOCK, MOBA_BLOCK), MOBA_BLOCK), cols]
            return _dot(kblk, qhb)

        def block_values(n):
            return vt_ref[hrows, pl.ds(pl.multiple_of(n * MOBA_BLOCK, MOBA_BLOCK), MOBA_BLOCK)]

        lg = jnp.where(causal, block_logits(own), NEG_INF)
        m0 = jnp.max(lg, axis=0, keepdims=True)
        p0 = jnp.exp(lg - m0)
        l0 = jnp.sum(p0, axis=0, keepdims=True)
        acc0 = _dot(block_values(own), p0.astype(BF16))

        def body(n, carry):
            m, l, acc = carry
            keep = sel_sc[pl.ds(n, 1), :] > 0.0
            lg = jnp.where(keep, block_logits(n), NEG_INF)
            m_new = jnp.maximum(m, jnp.max(lg, axis=0, keepdims=True))
            alpha = jnp.exp(m - m_new)
            pr = jnp.exp(lg - m_new)
            l = alpha * l + jnp.sum(pr, axis=0, keepdims=True)
            acc = alpha * acc + _dot(block_values(n), pr.astype(BF16))
            return m_new, l, acc

        _, l, acc = lax.fori_loop(0, own, body, (m0, l0, acc0))
        out_sc[hrows, :] = acc / l
    o_ref[...] = out_sc[...].T


def _moba_prompt(qt, kb, vt, kmean, *, n_seq, seq_len):
    nq = seq_len // Q_BLOCK
    nblk = seq_len // MOBA_BLOCK
    return pl.pallas_call(
        functools.partial(_moba_prompt_kernel, n_blocks=nblk),
        grid=(n_seq, nq),
        in_specs=[
            pl.BlockSpec((None, D_GROUP, Q_BLOCK), lambda b, i: (b, 0, i)),
            pl.BlockSpec((seq_len, D_GROUP), lambda b, i: (b, 0)),
            pl.BlockSpec((None, D_GROUP, seq_len), lambda b, i: (b, 0, 0)),
            pl.BlockSpec((None, nblk, D_GROUP), lambda b, i: (b, 0, 0)),
        ],
        out_specs=pl.BlockSpec((Q_BLOCK, D_GROUP), lambda b, i: (b * nq + i, 0)),
        out_shape=jax.ShapeDtypeStruct((n_seq * seq_len, D_GROUP), F32),
        scratch_shapes=[pltpu.VMEM((nblk, Q_BLOCK), F32), pltpu.VMEM((D_GROUP, Q_BLOCK), F32)],
        compiler_params=_cparams(("parallel", "arbitrary")),
        name="moba_prompt",
    )(qt, kb, vt, kmean.reshape(n_seq, nblk, D_GROUP))


PAGES_PER_BLOCK = MOBA_BLOCK // PAGE_SIZE


def _moba_sample_kernel(pt_ref, q_ref, kn_ref, vn_ref, *rest, n_past, n_new):
    k_refs = rest[:PAGES_PER_BLOCK]
    v_refs = rest[PAGES_PER_BLOCK:2 * PAGES_PER_BLOCK]
    o_ref, qbd_sc, m_sc, l_sc, acc_sc, g_sc = rest[2 * PAGES_PER_BLOCK:]
    n = pl.program_id(1)
    nrow = N_HEADS * n_new
    row_head = lax.broadcasted_iota(jnp.int32, (nrow, D_GROUP), 0) // n_new
    lane_head = lax.broadcasted_iota(jnp.int32, (nrow, D_GROUP), 1) // HEAD_DIM
    same_head = row_head == lane_head

    @pl.when(n == 0)
    def _():
        qt = jnp.concatenate([q_ref[...]] * N_HEADS, axis=0)
        qbd_sc[...] = jnp.where(same_head, qt * (HEAD_DIM ** -0.5), 0.0).astype(BF16)

    qbd = qbd_sc[...]
    lgs = [_dot(qbd, r[...].reshape(D_GROUP, PAGE_SIZE).astype(BF16)) for r in k_refs]
    m = lgs[0].max(axis=-1, keepdims=True)
    for lg in lgs[1:]:
        m = jnp.maximum(m, lg.max(axis=-1, keepdims=True))
    l = jnp.zeros_like(m)
    acc = jnp.zeros((nrow, D_GROUP), F32)
    gsum = jnp.zeros_like(m)
    for lg, r in zip(lgs, v_refs):
        pr = jnp.exp(lg - m)
        l = l + pr.sum(axis=-1, keepdims=True)
        acc = acc + _dot_nt(pr.astype(BF16), r[...].reshape(D_GROUP, PAGE_SIZE).astype(BF16))
        gsum = gsum + lg.sum(axis=-1, keepdims=True)
    m_sc[n] = m
    l_sc[n] = l
    acc_sc[n] = acc
    g_sc[n] = gsum

    @pl.when(n == n_past - 1)
    def _():
        rr = lax.broadcasted_iota(jnp.int32, (nrow, n_new), 0) % n_new
        cc = lax.broadcasted_iota(jnp.int32, (nrow, n_new), 1)
        lg_o = jnp.where(cc <= rr, _dot_nt(qbd, kn_ref[...].astype(BF16)), NEG_INF)
        m = jnp.max(lg_o, axis=-1, keepdims=True)
        p_o = jnp.exp(lg_o - m)
        l = jnp.sum(p_o, axis=-1, keepdims=True)
        acc = _dot(p_o.astype(BF16), vn_ref[...].astype(BF16))
        gates = [g_sc[b] for b in range(n_past)]
        taken = [jnp.zeros((nrow, 1), F32) for _ in range(n_past)]
        for _ in range(min(MOBA_TOPK, n_past)):
            best = gates[0]
            for b in range(1, n_past):
                best = jnp.maximum(best, gates[b])
            found = jnp.zeros((nrow, 1), F32)
            for b in range(n_past):
                hit = (gates[b] == best) & (found == 0.0)
                taken[b] = jnp.where(hit, 1.0, taken[b])
                found = jnp.where(hit, 1.0, found)
                gates[b] = jnp.where(hit, NEG_INF, gates[b])
        for b in range(n_past):
            m_b = jnp.where(taken[b] > 0.0, m_sc[b], NEG_INF)
            m_new = jnp.maximum(m, m_b)
            a_old = jnp.exp(m - m_new)
            a_b = jnp.exp(m_b - m_new)
            l = a_old * l + a_b * l_sc[b]
            acc = a_old * acc + a_b * acc_sc[b]
            m = m_new
        res = jnp.where(same_head, acc / l, 0.0)
        out = res[0:n_new]
        for h in range(1, N_HEADS):
            out = out + res[h * n_new:(h + 1) * n_new]
        o_ref[...] = out


def _moba_sample(q_m, k_new, v_new, cache_kt, cache_vt, page_table):
    n_seq, n_pages = page_table.shape
    n_new = q_m.shape[0] // n_seq
    nrow = n_new * N_HEADS
    n_past = n_pages // PAGES_PER_BLOCK
    new_spec = pl.BlockSpec((n_new, D_GROUP), lambda b, n, pt: (b, 0))

    def page_spec(j):
        return pl.BlockSpec((None, N_HEADS, HEAD_DIM, PAGE_SIZE),
                            lambda b, n, pt: (pt[b * n_pages + n * PAGES_PER_BLOCK + j], 0, 0, 0))

    pages = [page_spec(j) for j in range(PAGES_PER_BLOCK)]
    return pl.pallas_call(
        functools.partial(_moba_sample_kernel, n_past=n_past, n_new=n_new),
        grid_spec=pltpu.PrefetchScalarGridSpec(
            num_scalar_prefetch=1,
            grid=(n_seq, n_past),
            in_specs=[new_spec, new_spec, new_spec] + pages + pages,
            out_specs=new_spec,
            scratch_shapes=[pltpu.VMEM((nrow, D_GROUP), BF16), pltpu.VMEM((n_past, nrow, 1), F32),
                            pltpu.VMEM((n_past, nrow, 1), F32), pltpu.VMEM((n_past, nrow, D_GROUP), F32),
                            pltpu.VMEM((n_past, nrow, 1), F32)],
        ),
        out_shape=jax.ShapeDtypeStruct((n_seq * n_new, D_GROUP), F32),
        compiler_params=_cparams(("parallel", "arbitrary")),
        name="moba_sample",
    )(page_table.reshape(-1), q_m, k_new, v_new, *([cache_kt] * PAGES_PER_BLOCK), *([cache_vt] * PAGES_PER_BLOCK))


def _group_specs(width, tm, tiles_p):
    return (pl.BlockSpec((tm, width), lambda i: (jnp.minimum(i, tiles_p - 1), 0)),
            pl.BlockSpec((tm, width), lambda i: (jnp.maximum(i - tiles_p, 0), 0)))


def _pick(is_prompt, p_ref, s_ref):
    return jnp.where(is_prompt, p_ref[...], s_ref[...])


def _merge_kernel(xp_ref, xs_ref, op_ref, os_ref, gp_ref, gs_ref, mp_ref, ms_ref,
                  gnw_ref, gnb_ref, wout_ref, fnw_ref, rw_ref, rb_ref, avg_ref,
                  x1_ref, hn_ref, te_ref, gt_ref, *, tiles_p):
    is_p = pl.program_id(0) < tiles_p
    avg = avg_ref[...]
    o = _pick(is_p, op_ref, os_ref)
    mu = _split_dot(o, avg)
    d = o - mu
    var = _split_dot(d * d, avg)
    y = d * lax.rsqrt(var + EPS) * gnw_ref[...] + gnb_ref[...]
    g = _pick(is_p, gp_ref, gs_ref)
    ret = y * (g * jax.nn.sigmoid(g))
    x1 = (_pick(is_p, xp_ref, xs_ref) + _dot(ret.astype(BF16), wout_ref[:D_GROUP, :])
          + _dot(_pick(is_p, mp_ref, ms_ref).astype(BF16), wout_ref[D_GROUP:, :]))
    x1_ref[...] = x1
    hn = x1 * lax.rsqrt(jnp.mean(x1 * x1, axis=-1, keepdims=True) + EPS) * fnw_ref[...]
    hn_ref[...] = hn
    logits = jnp.dot(hn, rw_ref[...], preferred_element_type=F32, precision=lax.Precision.HIGHEST) + rb_ref[...]
    tm = logits.shape[0]
    eidx = lax.broadcasted_iota(jnp.int32, logits.shape, 1)
    lane = lax.broadcasted_iota(jnp.int32, (tm, LANES), 1)
    vals = jnp.zeros((tm, LANES), F32)
    idxs = jnp.zeros((tm, LANES), jnp.int32)
    top = None
    for t in range(TOP_K):
        m = jnp.max(logits, axis=-1, keepdims=True)
        idx = jnp.min(jnp.where(logits == m, eidx, N_EXPERTS), axis=-1, keepdims=True)
        if t == 0:
            top = m
        vals = jnp.where(lane == t, jnp.exp(m - top), vals)
        idxs = jnp.where(lane == t, idx, idxs)
        logits = jnp.where(eidx == idx, NEG_INF, logits)
    te_ref[...] = idxs
    gt_ref[...] = vals / jnp.sum(vals, axis=-1, keepdims=True)


def _merge(x, o_ret, g_r, o_moba, ret_gn_w, ret_gn_b, w_out_b, ffn_norm_w, router_w, router_b):
    n_p, d = x[0].shape
    n = n_p + x[1].shape[0]
    tm = TOKEN_TILE
    tiles_p = n_p // tm
    row = lambda i: (i, 0)
    full = lambda i: (0, 0)
    return pl.pallas_call(
        functools.partial(_merge_kernel, tiles_p=tiles_p),
        grid=(n // tm,),
        in_specs=[
            *_group_specs(d, tm, tiles_p), *_group_specs(D_GROUP, tm, tiles_p), *_group_specs(D_GROUP, tm, tiles_p),
            *_group_specs(D_GROUP, tm, tiles_p), pl.BlockSpec((1, D_GROUP), full), pl.BlockSpec((1, D_GROUP), full),
            pl.BlockSpec(w_out_b.shape, full), pl.BlockSpec((1, d), full), pl.BlockSpec(router_w.shape, full),
            pl.BlockSpec((1, N_EXPERTS), full), pl.BlockSpec((D_GROUP, D_GROUP), full),
        ],
        out_specs=[pl.BlockSpec((tm, d), row), pl.BlockSpec((tm, d), row),
                   pl.BlockSpec((tm, LANES), row), pl.BlockSpec((tm, LANES), row)],
        out_shape=[jax.ShapeDtypeStruct((n, d), F32), jax.ShapeDtypeStruct((n, d), F32),
                   jax.ShapeDtypeStruct((n, LANES), jnp.int32), jax.ShapeDtypeStruct((n, LANES), F32)],
        compiler_params=_cparams(("parallel",)),
        name="merge_router",
    )(*x, *o_ret, *g_r, *o_moba, ret_gn_w.reshape(1, -1), ret_gn_b.reshape(1, -1), w_out_b, ffn_norm_w.reshape(1, d),
      router_w, router_b.reshape(1, -1), _head_blockdiag(1.0 / HEAD_DIM))


def _experts_kernel(ge_ref, slot_ref, nreal_ref, hn_hbm, wg_ref, bg_ref, wu_ref, bu_ref, wd_ref, bd_ref, y_hbm,
                    xbuf, ybuf, wgb, wub, wdb, gsem, ssem, *, n_tokens):
    g = pl.program_id(0)
    n_real = nreal_ref[0]
    buf = g % 2

    def start_gather(grp, b):
        def body(r, c):
            slot = slot_ref[grp * MOE_ROWS + r]
            tok = jnp.maximum(slot, 0) // TOP_K
            pltpu.make_async_copy(hn_hbm.at[pl.ds(tok, 1), :], xbuf.at[b, pl.ds(r, 1), :], gsem.at[b]).start()
            return c
        lax.fori_loop(0, MOE_ROWS, body, 0)

    def start_scatter(grp, b):
        def body(r, c):
            slot = slot_ref[grp * MOE_ROWS + r]
            dst = jnp.where(slot >= 0, (slot % TOP_K) * n_tokens + slot // TOP_K, TOP_K * n_tokens + r)
            pltpu.make_async_copy(ybuf.at[b, pl.ds(r, 1), :], y_hbm.at[pl.ds(dst, 1), :], ssem.at[b]).start()
            return c
        lax.fori_loop(0, MOE_ROWS, body, 0)

    def wait_gather(b):
        pltpu.make_async_copy(hn_hbm.at[pl.ds(0, MOE_ROWS), :], xbuf.at[b], gsem.at[b]).wait()

    def wait_scatter(b):
        pltpu.make_async_copy(ybuf.at[b], y_hbm.at[pl.ds(0, MOE_ROWS), :], ssem.at[b]).wait()

    @pl.when(g == 0)
    def _():
        ybuf[0] = jnp.zeros((MOE_ROWS, ybuf.shape[-1]), F32)
        spare = pltpu.make_async_copy(ybuf.at[0], y_hbm.at[pl.ds(TOP_K * n_tokens, MOE_ROWS), :], ssem.at[0])
        spare.start()
        spare.wait()

    @pl.when((g == 0) & (n_real > 0))
    def _():
        start_gather(0, 0)

    @pl.when(g + 1 < n_real)
    def _():
        start_gather(g + 1, 1 - buf)

    @pl.when(g < n_real)
    def _():
        e_now = ge_ref[g]
        e_prev = ge_ref[jnp.maximum(g - 1, 0)]

        @pl.when((g == 0) | (e_now != e_prev))
        def _():
            wgb[...] = wg_ref[...].astype(BF16)
            wub[...] = wu_ref[...].astype(BF16)
            wdb[...] = wd_ref[...].astype(BF16)

        wait_gather(buf)

        @pl.when(g >= 2)
        def _():
            wait_scatter(buf)

        xb = xbuf[buf].astype(BF16)
        gate = jnp.minimum(_dot(xb, wgb[...]) + bg_ref[...], SWIGLU_LIMIT)
        up = jnp.clip(_dot(xb, wub[...]) + bu_ref[...], -SWIGLU_LIMIT, SWIGLU_LIMIT)
        act = (up + 1.0) * gate * jax.nn.sigmoid(SWIGLU_ALPHA * gate)
        ybuf[buf] = _dot(act.astype(BF16), wdb[...]) + bd_ref[...]
        start_scatter(g, buf)

    @pl.when((g == n_real - 1) & (n_real >= 2))
    def _():
        wait_scatter(1 - buf)

    @pl.when(g == n_real - 1)
    def _():
        wait_scatter(buf)


def _experts(hn, group_e, row_slot, n_real, w_gate, b_gate, w_up, b_up, w_down, b_down):
    n, d = hn.shape
    n_groups = group_e.shape[0]
    d_ff = w_gate.shape[-1]
    wspec = lambda shape: pl.BlockSpec((None,) + shape, lambda g, ge, rs, nr: (ge[g], 0, 0))
    return pl.pallas_call(
        functools.partial(_experts_kernel, n_tokens=n),
        grid_spec=pltpu.PrefetchScalarGridSpec(
            num_scalar_prefetch=3,
            grid=(n_groups,),
            in_specs=[pl.BlockSpec(memory_space=pl.ANY), wspec((d, d_ff)), wspec((1, d_ff)), wspec((d, d_ff)),
                      wspec((1, d_ff)), wspec((d_ff, d)), wspec((1, d))],
            out_specs=pl.BlockSpec(memory_space=pl.ANY),
            scratch_shapes=[pltpu.VMEM((2, MOE_ROWS, d), F32), pltpu.VMEM((2, MOE_ROWS, d), F32),
                            pltpu.VMEM((d, d_ff), BF16), pltpu.VMEM((d, d_ff), BF16), pltpu.VMEM((d_ff, d), BF16),
                            pltpu.SemaphoreType.DMA((2,)), pltpu.SemaphoreType.DMA((2,))],
        ),
        out_shape=jax.ShapeDtypeStruct((TOP_K * n + MOE_ROWS, d), F32),
        compiler_params=_cparams(("arbitrary",)),
        name="experts",
    )(group_e, row_slot, n_real, hn, w_gate, b_gate.reshape(N_EXPERTS, 1, d_ff), w_up, b_up.reshape(N_EXPERTS, 1, d_ff),
      w_down, b_down.reshape(N_EXPERTS, 1, d))


def _route(top_e):
    n = top_e.shape[0]
    n_assign = n * TOP_K
    flat_e = top_e.reshape(-1)
    order = jnp.argsort(flat_e).astype(jnp.int32)
    experts = jnp.arange(N_EXPERTS, dtype=jnp.int32)
    counts = jnp.sum((flat_e[:, None] == experts[None, :]).astype(jnp.int32), axis=0)
    starts = jnp.cumsum(counts) - counts
    padded = (counts + MOE_ROWS - 1) // MOE_ROWS * MOE_ROWS
    pad_ends = jnp.cumsum(padded)
    pad_starts = pad_ends - padded
    n_groups = -(-n_assign // MOE_ROWS) + N_EXPERTS
    g_start = jnp.arange(n_groups, dtype=jnp.int32) * MOE_ROWS
    group_e = jnp.minimum(jnp.sum((pad_ends[None, :] <= g_start[:, None]).astype(jnp.int32), axis=1), N_EXPERTS - 1)
    offset = (pad_starts - starts)[group_e]
    end = (starts + counts)[group_e]
    j = g_start[:, None] + jnp.arange(MOE_ROWS, dtype=jnp.int32)[None, :] - offset[:, None]
    row_slot = jnp.where(j < end[:, None], order[jnp.clip(j, 0, n_assign - 1)], -1)
    n_real = (pad_ends[-1] // MOE_ROWS).reshape(1)
    return group_e.astype(jnp.int32), row_slot.reshape(-1).astype(jnp.int32), n_real.astype(jnp.int32)


def _final_kernel(x1_ref, *rest, tiles_p):
    y_refs = rest[:TOP_K]
    gt_ref, pp_ref, ps_ref, nw_ref, wg_ref, wp_ref, op_ref, os_ref = rest[TOP_K:]
    i = pl.program_id(0)
    is_p = i < tiles_p
    gt = gt_ref[...]
    x2 = x1_ref[...]
    for k in range(TOP_K):
        x2 = x2 + y_refs[k][...] * gt[:, k:k + 1]
    hn = x2 * lax.rsqrt(jnp.mean(x2 * x2, axis=-1, keepdims=True) + EPS) * nw_ref[...]
    gate = jax.nn.sigmoid(_dot(hn.astype(BF16), wg_ref[...]))
    out = x2 + gate * _dot(_pick(is_p, pp_ref, ps_ref).astype(BF16), wp_ref[...])

    @pl.when(is_p)
    def _():
        op_ref[...] = out

    @pl.when(jnp.logical_not(is_p))
    def _():
        os_ref[...] = out


def _final(x1, y, gates, p, pl_norm_w, pl_gate_b, pl_proj_b):
    n, d = x1.shape
    n_p = p[0].shape[0]
    tm = TOKEN_TILE
    tiles_p = n_p // tm
    tiles = n // tm
    row = lambda i: (i, 0)
    full = lambda i: (0, 0)
    y_specs = [pl.BlockSpec((tm, d), functools.partial(lambda i, k: (k * tiles + i, 0), k=k)) for k in range(TOP_K)]
    return pl.pallas_call(
        functools.partial(_final_kernel, tiles_p=tiles_p),
        grid=(tiles,),
        in_specs=[pl.BlockSpec((tm, d), row), *y_specs,
                  pl.BlockSpec((tm, LANES), row), *_group_specs(p[0].shape[-1], tm, tiles_p),
                  pl.BlockSpec((1, d), full), pl.BlockSpec(pl_gate_b.shape, full), pl.BlockSpec(pl_proj_b.shape, full)],
        out_specs=list(_group_specs(d, tm, tiles_p)),
        out_shape=[jax.ShapeDtypeStruct((n_p, d), F32), jax.ShapeDtypeStruct((n - n_p, d), F32)],
        compiler_params=_cparams(("arbitrary",)),
        name="final",
    )(x1, *([y] * TOP_K), gates, *p, pl_norm_w.reshape(1, d), pl_gate_b, pl_proj_b)


def kernel(x_prompt, x_sample, cache_k, cache_v, page_table, state_ret, p_prompt, p_sample, attn_norm_w, w_in, q_norm_w, k_norm_w, ret_gn_w, ret_gn_b, w_out, ffn_norm_w, router_w, router_b, w_gate, b_gate, w_up, b_up, w_down, b_down, pl_norm_w, pl_gate_w, pl_proj_w):
    depth = w_in.shape[0]
    b, s, d = x_prompt.shape
    bd, sd, _ = x_sample.shape
    n_p = b * s
    n_s = bd * sd
    past_len = page_table.shape[1] * PAGE_SIZE
    pos_p = jnp.arange(s, dtype=jnp.int32)
    pos_s = past_len + jnp.arange(sd, dtype=jnp.int32)
    xp = x_prompt.reshape(n_p, d)
    xs = x_sample.reshape(n_s, d)
    heads_last = lambda t: jnp.transpose(t.reshape(b, N_HEADS, HEAD_DIM, s), (0, 3, 1, 2))
    kp_l, vp_l, sp_l, ks_l, vs_l, ss_l = [], [], [], [], [], []
    for i in range(depth):
        w_in_b = w_in[i].astype(BF16)
        qr, kr, vr, gr_p, qt, kt, vt, kb, vtb, kmean = _proj(
            xp, pos_p, attn_norm_w[i], w_in_b, q_norm_w[i], k_norm_w[i], prompt=True, seq_len=s)
        oret_p, st_p = _retention(qr, kr, vr, n_seq=b, L=RET_CHUNK, n_chunks=s // RET_CHUNK, seqs_per_step=1)
        omoba_p = _moba_prompt(qt, kb, vtb, kmean, n_seq=b, seq_len=s)
        qr, kr, vr, gr_s, qm_s, km_s, vm_s = _proj(
            xs, pos_s, attn_norm_w[i], w_in_b, q_norm_w[i], k_norm_w[i], prompt=False)
        oret_s, st_s = _retention(qr, kr, vr, n_seq=bd, L=sd, n_chunks=1, seqs_per_step=8, init_state=state_ret[i])
        omoba_s = _moba_sample(qm_s, km_s, vm_s, jnp.transpose(cache_k[i], (0, 2, 3, 1)),
                               jnp.transpose(cache_v[i], (0, 2, 3, 1)), page_table)
        x1, hn, top_e, gates = _merge(
            (xp, xs), (oret_p, oret_s), (gr_p, gr_s), (omoba_p, omoba_s), ret_gn_w[i], ret_gn_b[i],
            w_out[i].astype(BF16), ffn_norm_w[i], router_w[i], router_b[i])
        group_e, row_slot, n_real = _route(top_e[:, :TOP_K])
        y = _experts(hn, group_e, row_slot, n_real, w_gate[i], b_gate[i], w_up[i], b_up[i], w_down[i], b_down[i])
        xp, xs = _final(x1, y, gates,
                        (p_prompt[i].reshape(n_p, -1), p_sample[i].reshape(n_s, -1)),
                        pl_norm_w[i], pl_gate_w[i].astype(BF16), pl_proj_w[i].astype(BF16))
        kp_l.append(heads_last(kt))
        vp_l.append(heads_last(vt))
        sp_l.append(st_p)
        ks_l.append(km_s.reshape(bd, sd, N_HEADS, HEAD_DIM))
        vs_l.append(vm_s.reshape(bd, sd, N_HEADS, HEAD_DIM))
        ss_l.append(st_s)
    return (xp.reshape(b, s, d), xs.reshape(bd, sd, d), jnp.stack(kp_l), jnp.stack(vp_l), jnp.stack(sp_l),
            jnp.stack(ks_l), jnp.stack(vs_l), jnp.stack(ss_l))
```

```python
import functools

import jax
import jax.numpy as jnp
import numpy as np
from jax import lax
from jax.experimental import pallas as pl
from jax.experimental.pallas import tpu as pltpu

F32 = jnp.float32
BF16 = jnp.bfloat16

HEAD_DIM = 64
N_HEADS = 8
D_GROUP = N_HEADS * HEAD_DIM
RET_CHUNK = 128
MOBA_BLOCK = 256
MOBA_TOPK = 3
Q_BLOCK = 128
PAGE_SIZE = 128
N_EXPERTS = 32
TOP_K = 4
SWIGLU_LIMIT = 7.0
SWIGLU_ALPHA = 1.702
ROPE_BASE = 10000.0
EPS = 1e-6
NEG_INF = float("-inf")

LANES = 128
PROJ_TILE = 256
TOKEN_TILE = 256
MOE_ROWS = 128
VMEM_LIMIT = 56 * 1024 * 1024


def _cparams(sem, vmem=VMEM_LIMIT):
    return pltpu.CompilerParams(dimension_semantics=sem, vmem_limit_bytes=vmem)


def _dot(a, b):
    return jnp.dot(a, b, preferred_element_type=F32)


def _dot_nt(a, b):
    return lax.dot_general(a, b, (((1,), (1,)), ((), ())), preferred_element_type=F32)


def _split_dot(x, a):
    hi = x.astype(BF16)
    lo = (x - hi.astype(F32)).astype(BF16)
    return _dot(hi, a) + _dot(lo, a)


def _head_blockdiag(scale):
    i = np.arange(D_GROUP) // HEAD_DIM
    return jnp.asarray((i[:, None] == i[None, :]).astype(np.float32) * scale, BF16)


def _proj_kernel(x_ref, nw_ref, w_ref, qn_ref, kn_ref, cos_ref, sin_ref, bd_ref, *out_refs, prompt):
    x = x_ref[...]
    h = x * lax.rsqrt(jnp.mean(x * x, axis=-1, keepdims=True) + EPS) * nw_ref[...]
    hb = h.astype(BF16)
    cos = cos_ref[...]
    sin = sin_ref[...]
    lane = lax.broadcasted_iota(jnp.int32, (1, D_GROUP), 1)
    first_half = (lane % HEAD_DIM) < (HEAD_DIM // 2)

    def zcol(c):
        return _dot(hb, w_ref[:, c * D_GROUP:(c + 1) * D_GROUP])

    def rot(z):
        partner = jnp.where(first_half, pltpu.roll(z, D_GROUP - HEAD_DIM // 2, 1), pltpu.roll(z, HEAD_DIM // 2, 1))
        return z * cos + partner * sin

    def qk_norm(z, w):
        ms = _split_dot(z * z, bd_ref[...])
        return z * lax.rsqrt(ms + EPS) * w

    q_r = rot(zcol(0))
    k_r = rot(zcol(1)) * (HEAD_DIM ** -0.5)
    v_r = zcol(2)
    g_r = zcol(3)
    q_m = qk_norm(zcol(4), qn_ref[...])
    k_m = qk_norm(zcol(5), kn_ref[...])
    v_m = zcol(6)
    if prompt:
        qr_ref, kr_ref, vr_ref, gr_ref, qt_ref, kt_ref, vt_ref, kb_ref, vtb_ref, kmean_ref = out_refs
        qt_ref[...] = q_m.T
        kt_ref[...] = k_m.T
        v_t = v_m.T
        vt_ref[...] = v_t
        vtb_ref[...] = v_t.astype(BF16)
        kb_ref[...] = k_m.astype(BF16)
        kmean_ref[...] = jnp.mean(k_m, axis=0, keepdims=True)
    else:
        qr_ref, kr_ref, vr_ref, gr_ref, qm_ref, km_ref, vm_ref = out_refs
        qm_ref[...] = q_m
        km_ref[...] = k_m
        vm_ref[...] = v_m
    qr_ref[...] = q_r
    kr_ref[...] = k_r
    vr_ref[...] = v_r
    gr_ref[...] = g_r


def _rope_tables(pos):
    half = HEAD_DIM // 2
    inv = ROPE_BASE ** (-jnp.arange(half, dtype=F32) / half)
    ang = pos.astype(F32)[:, None] * inv[None, :]
    cos = jnp.cos(ang)
    sin = jnp.sin(ang)
    cos_h = jnp.concatenate([cos, cos], axis=-1)
    sin_h = jnp.concatenate([-sin, sin], axis=-1)
    return jnp.tile(cos_h, (1, N_HEADS)), jnp.tile(sin_h, (1, N_HEADS))


def _proj(x2d, pos_table, attn_norm_w, w_in_b, q_norm_w, k_norm_w, *, prompt, seq_len=None):
    n, d = x2d.shape
    tm = PROJ_TILE
    cos, sin = _rope_tables(pos_table)
    p = pos_table.shape[0]
    if p < tm:
        cos = jnp.tile(cos, (tm // p, 1))
        sin = jnp.tile(sin, (tm // p, 1))
        p = tm
    tabs = p // tm
    nt = n // tm
    row = lambda i: (i, 0)
    full = lambda i: (0, 0)
    in_specs = [
        pl.BlockSpec((tm, d), row),
        pl.BlockSpec((1, d), full),
        pl.BlockSpec(w_in_b.shape, full),
        pl.BlockSpec((1, D_GROUP), full),
        pl.BlockSpec((1, D_GROUP), full),
        pl.BlockSpec((tm, D_GROUP), lambda i: (i % tabs, 0)),
        pl.BlockSpec((tm, D_GROUP), lambda i: (i % tabs, 0)),
        pl.BlockSpec((D_GROUP, D_GROUP), full),
    ]
    tok = jax.ShapeDtypeStruct((n, D_GROUP), F32)
    tok_spec = pl.BlockSpec((tm, D_GROUP), row)
    if prompt:
        nb = n // seq_len
        tps = seq_len // tm
        t_spec = pl.BlockSpec((None, D_GROUP, tm), lambda i: (i // tps, 0, i % tps))
        t_f32 = jax.ShapeDtypeStruct((nb, D_GROUP, seq_len), F32)
        out_shape = [tok] * 4 + [t_f32] * 3 + [
            jax.ShapeDtypeStruct((n, D_GROUP), BF16),
            jax.ShapeDtypeStruct((nb, D_GROUP, seq_len), BF16),
            jax.ShapeDtypeStruct((nt, 1, D_GROUP), F32),
        ]
        out_specs = [tok_spec] * 4 + [t_spec] * 3 + [tok_spec, t_spec,
                                                     pl.BlockSpec((None, 1, D_GROUP), lambda i: (i, 0, 0))]
    else:
        out_shape = [tok] * 7
        out_specs = [tok_spec] * 7
    return pl.pallas_call(
        functools.partial(_proj_kernel, prompt=prompt),
        grid=(nt,),
        in_specs=in_specs,
        out_specs=out_specs,
        out_shape=out_shape,
        compiler_params=_cparams(("parallel",)),
        name="proj_prompt" if prompt else "proj_sample",
    )(x2d, attn_norm_w.reshape(1, d), w_in_b, jnp.tile(q_norm_w, N_HEADS).reshape(1, D_GROUP),
      jnp.tile(k_norm_w, N_HEADS).reshape(1, D_GROUP), cos, sin, _head_blockdiag(1.0 / HEAD_DIM))


N_PAIRS = N_HEADS // 2


def _retention_log_decay():
    return jnp.log1p(-jnp.exp2(-5.0 - jnp.arange(N_HEADS, dtype=F32)))


def _retention_tables(L):
    lg = _retention_log_decay()
    i = jnp.arange(L, dtype=F32)
    diff = i[:, None] - i[None, :]
    dec = jnp.where(diff >= 0, jnp.exp(lg[:, None, None] * jnp.maximum(diff, 0.0)), 0.0)
    cross = jnp.repeat(jnp.exp(lg[:, None] * (i + 1.0)).T, HEAD_DIM, axis=1)
    kdec = jnp.repeat(jnp.exp(lg[:, None] * (L - 1.0 - i)).T, HEAD_DIM, axis=1)
    sg = jnp.exp(lg * L)
    blk = np.arange(LANES) // HEAD_DIM
    bd = jnp.asarray((blk[:, None] == blk[None, :]).astype(np.float32))
    sg_lane = jnp.repeat(sg.reshape(N_PAIRS, 2), HEAD_DIM, axis=1)
    sgam = sg_lane[:, :, None] * bd[None]
    return dec, cross, kdec, sgam, bd


def _retention_kernel(q_ref, k_ref, v_ref, s0_ref, dec_ref, cross_ref, kdec_ref, sgam_ref, bd_ref,
                      o_ref, st_ref, s_sc, *, L, seqs, use_init):
    c = pl.program_id(1)

    @pl.when(c == 0)
    def _():
        if use_init:
            s_sc[...] = s0_ref[...]
        else:
            s_sc[...] = jnp.zeros_like(s_sc)

    lane = lax.broadcasted_iota(jnp.int32, (1, LANES), 1)
    head0 = lane < HEAD_DIM
    bd = bd_ref[...]
    for s in range(seqs):
        rows = slice(s * L, (s + 1) * L)
        for p in range(N_PAIRS):
            cols = slice(p * LANES, (p + 1) * LANES)
            q2 = q_ref[rows, cols]
            k2 = k_ref[rows, cols]
            v2 = v_ref[rows, cols]
            q2b = q2.astype(BF16)
            v2b = v2.astype(BF16)
            state = s_sc[s, p]
            o = _dot(q2b, state.astype(BF16)) * cross_ref[:, cols]
            for j in range(2):
                mj = head0 if j == 0 else jnp.logical_not(head0)
                kj = jnp.where(mj, k2, 0.0).astype(BF16)
                sc = _dot_nt(q2b, kj) * dec_ref[2 * p + j]
                o = o + jnp.where(mj, _dot(sc.astype(BF16), v2b), 0.0)
            o_ref[rows, cols] = o
            kd = (k2 * kdec_ref[:, cols]).astype(BF16)
            upd = lax.dot_general(kd, v2b, (((0,), (0,)), ((), ())), preferred_element_type=F32)
            s_sc[s, p] = sgam_ref[p] * state + bd * upd

    @pl.when(c == pl.num_programs(1) - 1)
    def _():
        st_ref[...] = s_sc[...]


def _pair_states(state):
    n = state.shape[0]
    s = state.reshape(n, N_PAIRS, 2, HEAD_DIM, HEAD_DIM)
    z = jnp.zeros_like(s[:, :, 0])
    top = jnp.concatenate([s[:, :, 0], z], axis=-1)
    bot = jnp.concatenate([z, s[:, :, 1]], axis=-1)
    return jnp.concatenate([top, bot], axis=-2)


def _unpair_states(sp):
    n = sp.shape[0]
    a = sp[:, :, :HEAD_DIM, :HEAD_DIM]
    b = sp[:, :, HEAD_DIM:, HEAD_DIM:]
    return jnp.stack([a, b], axis=2).reshape(n, N_HEADS, HEAD_DIM, HEAD_DIM)


def _retention(q, k, v, *, n_seq, L, n_chunks, seqs_per_step, init_state=None):
    n = q.shape[0]
    assert n == n_seq * n_chunks * L and n_seq % seqs_per_step == 0
    assert seqs_per_step == 1 or n_chunks == 1
    dec, cross, kdec, sgam, bd = _retention_tables(L)
    use_init = init_state is not None
    s0 = _pair_states(init_state.astype(F32)) if use_init else jnp.zeros((n_seq, N_PAIRS, LANES, LANES), F32)
    sb = seqs_per_step
    rows = sb * L
    tok_spec = pl.BlockSpec((rows, D_GROUP), lambda b, c: (b * n_chunks + c, 0))
    st_spec = pl.BlockSpec((sb, N_PAIRS, LANES, LANES), lambda b, c: (b, 0, 0, 0))
    const = lambda shape: pl.BlockSpec(shape, lambda b, c: (0,) * len(shape))
    o, st = pl.pallas_call(
        functools.partial(_retention_kernel, L=L, seqs=sb, use_init=use_init),
        grid=(n_seq // sb, n_chunks),
        in_specs=[tok_spec, tok_spec, tok_spec, st_spec, const(dec.shape), const(cross.shape), const(kdec.shape),
                  const(sgam.shape), const(bd.shape)],
        out_specs=[tok_spec, st_spec],
        out_shape=[jax.ShapeDtypeStruct((n, D_GROUP), F32), jax.ShapeDtypeStruct(s0.shape, F32)],
        scratch_shapes=[pltpu.VMEM((sb, N_PAIRS, LANES, LANES), F32)],
        compiler_params=_cparams(("parallel", "arbitrary")),
        name="retention_sample" if use_init else "retention_prompt",
    )(q, k, v, s0, dec, cross, kdec, sgam, bd)
    return o, _unpair_states(st)


def _top_blocks_t(gate_t, n_valid, rows):
    blk = lax.broadcasted_iota(jnp.int32, gate_t.shape, 0)
    g = jnp.where(blk < n_valid, gate_t, NEG_INF)
    sel = jnp.zeros(gate_t.shape, F32)
    for _ in range(MOBA_TOPK):
        m = jnp.max(g, axis=0, keepdims=True)
        idx = jnp.min(jnp.where(g == m, blk, rows), axis=0, keepdims=True)
        hit = (blk == idx) & (blk < n_valid)
        sel = jnp.where(hit, 1.0, sel)
        g = jnp.where(blk == idx, NEG_INF, g)
    return sel


def _moba_prompt_kernel(qt_ref, kb_ref, vt_ref, kmean_ref, o_ref, sel_sc, q_sc, lg_sc, p_sc, acc_sc, *, n_blocks):
    qi = pl.program_id(1)
    own = qi // (MOBA_BLOCK // Q_BLOCK)
    sub = lax.broadcasted_iota(jnp.int32, (LANES, 1), 0)
    key_row = lax.broadcasted_iota(jnp.int32, (MOBA_BLOCK, Q_BLOCK), 0)
    q_col = lax.broadcasted_iota(jnp.int32, (MOBA_BLOCK, Q_BLOCK), 1)
    causal = (own * MOBA_BLOCK + key_row) <= (qi * Q_BLOCK + q_col)

    for p in range(N_PAIRS):
        q2t = qt_ref[p * LANES:(p + 1) * LANES, :]
        for j in range(2):
            in_head = (sub >= j * HEAD_DIM) & (sub < (j + 1) * HEAD_DIM)
            qht = jnp.where(in_head, q2t, 0.0)
            gate_t = jnp.dot(kmean_ref[:, p * LANES:(p + 1) * LANES], qht, preferred_element_type=F32,
                             precision=lax.Precision.HIGHEST)
            sel_sc[2 * p + j] = _top_blocks_t(gate_t, own, n_blocks)
            q_sc[p, :, j * Q_BLOCK:(j + 1) * Q_BLOCK] = (qht * (HEAD_DIM ** -0.5)).astype(BF16)

    def block(n, ms, ls, mask_of):
        start = pl.multiple_of(n * MOBA_BLOCK, MOBA_BLOCK)
        for p in range(N_PAIRS):
            lg_sc[p] = _dot(kb_ref[pl.ds(start, MOBA_BLOCK), p * LANES:(p + 1) * LANES], q_sc[p])
        ms_new, ls_new, alphas = [], [], []
        for h in range(N_HEADS):
            p, j = divmod(h, 2)
            lg = jnp.where(mask_of(h), lg_sc[p, :, j * Q_BLOCK:(j + 1) * Q_BLOCK], NEG_INF)
            m_blk = jnp.max(lg, axis=0, keepdims=True)
            m_new = m_blk if ms is None else jnp.maximum(ms[h], m_blk)
            pr = jnp.exp(lg - m_new)
            p_sc[h] = pr.astype(BF16)
            l_blk = jnp.sum(pr, axis=0, keepdims=True)
            if ms is None:
                alphas.append(None)
                ls_new.append(l_blk)
            else:
                alpha = jnp.exp(ms[h] - m_new)
                alphas.append(alpha)
                ls_new.append(alpha * ls[h] + l_blk)
            ms_new.append(m_new)
        for h in range(N_HEADS):
            pv = _dot(vt_ref[h * HEAD_DIM:(h + 1) * HEAD_DIM, pl.ds(start, MOBA_BLOCK)], p_sc[h])
            acc_sc[h] = pv if ms is None else alphas[h] * acc_sc[h] + pv
        return tuple(ms_new), tuple(ls_new)

    ms, ls = block(own, None, None, lambda h: causal)

    def body(n, carry):
        return block(n, carry[0], carry[1], lambda h: sel_sc[h, pl.ds(n, 1), :] > 0.0)

    _, ls = lax.fori_loop(0, own, body, (ms, ls))
    for h in range(N_HEADS):
        acc_sc[h] = acc_sc[h] / ls[h]
    o_ref[...] = acc_sc[...].reshape(D_GROUP, Q_BLOCK).T


def _moba_prompt(qt, kb, vt, kmean, *, n_seq, seq_len):
    nq = seq_len // Q_BLOCK
    nblk = seq_len // MOBA_BLOCK
    return pl.pallas_call(
        functools.partial(_moba_prompt_kernel, n_blocks=nblk),
        grid=(n_seq, nq),
        in_specs=[
            pl.BlockSpec((None, D_GROUP, Q_BLOCK), lambda b, i: (b, 0, i)),
            pl.BlockSpec((seq_len, D_GROUP), lambda b, i: (b, 0)),
            pl.BlockSpec((None, D_GROUP, seq_len), lambda b, i: (b, 0, 0)),
            pl.BlockSpec((None, nblk, D_GROUP), lambda b, i: (b, 0, 0)),
        ],
        out_specs=pl.BlockSpec((Q_BLOCK, D_GROUP), lambda b, i: (b * nq + i, 0)),
        out_shape=jax.ShapeDtypeStruct((n_seq * seq_len, D_GROUP), F32),
        scratch_shapes=[pltpu.VMEM((N_HEADS, nblk, Q_BLOCK), F32), pltpu.VMEM((N_PAIRS, LANES, 2 * Q_BLOCK), BF16),
                        pltpu.VMEM((N_PAIRS, MOBA_BLOCK, 2 * Q_BLOCK), F32), pltpu.VMEM((N_HEADS, MOBA_BLOCK, Q_BLOCK), BF16),
                        pltpu.VMEM((N_HEADS, HEAD_DIM, Q_BLOCK), F32)],
        compiler_params=_cparams(("parallel", "arbitrary")),
        name="moba_prompt",
    )(qt, kb, vt, kmean.reshape(n_seq, nblk, D_GROUP))


PAGES_PER_BLOCK = MOBA_BLOCK // PAGE_SIZE


def _moba_sample_kernel(pt_ref, q_ref, kn_ref, vn_ref, *rest, n_past, n_new):
    k_refs = rest[:PAGES_PER_BLOCK]
    v_refs = rest[PAGES_PER_BLOCK:2 * PAGES_PER_BLOCK]
    o_ref, qbd_sc, m_sc, l_sc, acc_sc, g_sc = rest[2 * PAGES_PER_BLOCK:]
    n = pl.program_id(1)
    nrow = N_HEADS * n_new
    row_head = lax.broadcasted_iota(jnp.int32, (nrow, D_GROUP), 0) // n_new
    lane_head = lax.broadcasted_iota(jnp.int32, (nrow, D_GROUP), 1) // HEAD_DIM
    same_head = row_head == lane_head

    @pl.when(n == 0)
    def _():
        qt = jnp.concatenate([q_ref[...]] * N_HEADS, axis=0)
        qbd_sc[...] = jnp.where(same_head, qt * (HEAD_DIM ** -0.5), 0.0).astype(BF16)

    qbd = qbd_sc[...]
    lgs = [_dot(qbd, r[...].reshape(D_GROUP, PAGE_SIZE).astype(BF16)) for r in k_refs]
    m = lgs[0].max(axis=-1, keepdims=True)
    for lg in lgs[1:]:
        m = jnp.maximum(m, lg.max(axis=-1, keepdims=True))
    l = jnp.zeros_like(m)
    acc = jnp.zeros((nrow, D_GROUP), F32)
    gsum = jnp.zeros_like(m)
    for lg, r in zip(lgs, v_refs):
        pr = jnp.exp(lg - m)
        l = l + pr.sum(axis=-1, keepdims=True)
        acc = acc + _dot_nt(pr.astype(BF16), r[...].reshape(D_GROUP, PAGE_SIZE).astype(BF16))
        gsum = gsum + lg.sum(axis=-1, keepdims=True)
    m_sc[n] = m
    l_sc[n] = l
    acc_sc[n] = acc
    g_sc[n] = gsum

    @pl.when(n == n_past - 1)
    def _():
        rr = lax.broadcasted_iota(jnp.int32, (nrow, n_new), 0) % n_new
        cc = lax.broadcasted_iota(jnp.int32, (nrow, n_new), 1)
        lg_o = jnp.where(cc <= rr, _dot_nt(qbd, kn_ref[...].astype(BF16)), NEG_INF)
        m = jnp.max(lg_o, axis=-1, keepdims=True)
        p_o = jnp.exp(lg_o - m)
        l = jnp.sum(p_o, axis=-1, keepdims=True)
        acc = _dot(p_o.astype(BF16), vn_ref[...].astype(BF16))
        gates = [g_sc[b] for b in range(n_past)]
        taken = [jnp.zeros((nrow, 1), F32) for _ in range(n_past)]
        for _ in range(min(MOBA_TOPK, n_past)):
            best = gates[0]
            for b in range(1, n_past):
                best = jnp.maximum(best, gates[b])
            found = jnp.zeros((nrow, 1), F32)
            for b in range(n_past):
                hit = (gates[b] == best) & (found == 0.0)
                taken[b] = jnp.where(hit, 1.0, taken[b])
                found = jnp.where(hit, 1.0, found)
                gates[b] = jnp.where(hit, NEG_INF, gates[b])
        for b in range(n_past):
            m_b = jnp.where(taken[b] > 0.0, m_sc[b], NEG_INF)
            m_new = jnp.maximum(m, m_b)
            a_old = jnp.exp(m - m_new)
            a_b = jnp.exp(m_b - m_new)
            l = a_old * l + a_b * l_sc[b]
            acc = a_old * acc + a_b * acc_sc[b]
            m = m_new
        res = jnp.where(same_head, acc / l, 0.0)
        out = res[0:n_new]
        for h in range(1, N_HEADS):
            out = out + res[h * n_new:(h + 1) * n_new]
        o_ref[...] = out


def _moba_sample(q_m, k_new, v_new, cache_kt, cache_vt, page_table):
    n_seq, n_pages = page_table.shape
    n_new = q_m.shape[0] // n_seq
    nrow = n_new * N_HEADS
    n_past = n_pages // PAGES_PER_BLOCK
    new_spec = pl.BlockSpec((n_new, D_GROUP), lambda b, n, pt: (b, 0))

    def page_spec(j):
        return pl.BlockSpec((None, N_HEADS, HEAD_DIM, PAGE_SIZE),
                            lambda b, n, pt: (pt[b * n_pages + n * PAGES_PER_BLOCK + j], 0, 0, 0))

    pages = [page_spec(j) for j in range(PAGES_PER_BLOCK)]
    return pl.pallas_call(
        functools.partial(_moba_sample_kernel, n_past=n_past, n_new=n_new),
        grid_spec=pltpu.PrefetchScalarGridSpec(
            num_scalar_prefetch=1,
            grid=(n_seq, n_past),
            in_specs=[new_spec, new_spec, new_spec] + pages + pages,
            out_specs=new_spec,
            scratch_shapes=[pltpu.VMEM((nrow, D_GROUP), BF16), pltpu.VMEM((n_past, nrow, 1), F32),
                            pltpu.VMEM((n_past, nrow, 1), F32), pltpu.VMEM((n_past, nrow, D_GROUP), F32),
                            pltpu.VMEM((n_past, nrow, 1), F32)],
        ),
        out_shape=jax.ShapeDtypeStruct((n_seq * n_new, D_GROUP), F32),
        compiler_params=_cparams(("parallel", "arbitrary")),
        name="moba_sample",
    )(page_table.reshape(-1), q_m, k_new, v_new, *([cache_kt] * PAGES_PER_BLOCK), *([cache_vt] * PAGES_PER_BLOCK))


def _group_specs(width, tm, tiles_p):
    return (pl.BlockSpec((tm, width), lambda i: (jnp.minimum(i, tiles_p - 1), 0)),
            pl.BlockSpec((tm, width), lambda i: (jnp.maximum(i - tiles_p, 0), 0)))


def _pick(is_prompt, p_ref, s_ref):
    return jnp.where(is_prompt, p_ref[...], s_ref[...])


def _merge_kernel(xp_ref, xs_ref, op_ref, os_ref, gp_ref, gs_ref, mp_ref, ms_ref,
                  gnw_ref, gnb_ref, wout_ref, fnw_ref, rw_ref, rb_ref, avg_ref,
                  x1_ref, hn_ref, te_ref, gt_ref, *, tiles_p):
    is_p = pl.program_id(0) < tiles_p
    avg = avg_ref[...]
    o = _pick(is_p, op_ref, os_ref)
    mu = _split_dot(o, avg)
    d = o - mu
    var = _split_dot(d * d, avg)
    y = d * lax.rsqrt(var + EPS) * gnw_ref[...] + gnb_ref[...]
    g = _pick(is_p, gp_ref, gs_ref)
    ret = y * (g * jax.nn.sigmoid(g))
    x1 = (_pick(is_p, xp_ref, xs_ref) + _dot(ret.astype(BF16), wout_ref[:D_GROUP, :])
          + _dot(_pick(is_p, mp_ref, ms_ref).astype(BF16), wout_ref[D_GROUP:, :]))
    x1_ref[...] = x1
    hn = x1 * lax.rsqrt(jnp.mean(x1 * x1, axis=-1, keepdims=True) + EPS) * fnw_ref[...]
    hn_ref[...] = hn
    logits = jnp.dot(hn, rw_ref[...], preferred_element_type=F32, precision=lax.Precision.HIGHEST) + rb_ref[...]
    tm = logits.shape[0]
    eidx = lax.broadcasted_iota(jnp.int32, logits.shape, 1)
    lane = lax.broadcasted_iota(jnp.int32, (tm, LANES), 1)
    vals = jnp.zeros((tm, LANES), F32)
    idxs = jnp.zeros((tm, LANES), jnp.int32)
    top = None
    for t in range(TOP_K):
        m = jnp.max(logits, axis=-1, keepdims=True)
        idx = jnp.min(jnp.where(logits == m, eidx, N_EXPERTS), axis=-1, keepdims=True)
        if t == 0:
            top = m
        vals = jnp.where(lane == t, jnp.exp(m - top), vals)
        idxs = jnp.where(lane == t, idx, idxs)
        logits = jnp.where(eidx == idx, NEG_INF, logits)
    te_ref[...] = idxs
    gt_ref[...] = vals / jnp.sum(vals, axis=-1, keepdims=True)


def _merge(x, o_ret, g_r, o_moba, ret_gn_w, ret_gn_b, w_out_b, ffn_norm_w, router_w, router_b):
    n_p, d = x[0].shape
    n = n_p + x[1].shape[0]
    tm = TOKEN_TILE
    tiles_p = n_p // tm
    row = lambda i: (i, 0)
    full = lambda i: (0, 0)
    return pl.pallas_call(
        functools.partial(_merge_kernel, tiles_p=tiles_p),
        grid=(n // tm,),
        in_specs=[
            *_group_specs(d, tm, tiles_p), *_group_specs(D_GROUP, tm, tiles_p), *_group_specs(D_GROUP, tm, tiles_p),
            *_group_specs(D_GROUP, tm, tiles_p), pl.BlockSpec((1, D_GROUP), full), pl.BlockSpec((1, D_GROUP), full),
            pl.BlockSpec(w_out_b.shape, full), pl.BlockSpec((1, d), full), pl.BlockSpec(router_w.shape, full),
            pl.BlockSpec((1, N_EXPERTS), full), pl.BlockSpec((D_GROUP, D_GROUP), full),
        ],
        out_specs=[pl.BlockSpec((tm, d), row), pl.BlockSpec((tm, d), row),
                   pl.BlockSpec((tm, LANES), row), pl.BlockSpec((tm, LANES), row)],
        out_shape=[jax.ShapeDtypeStruct((n, d), F32), jax.ShapeDtypeStruct((n, d), F32),
                   jax.ShapeDtypeStruct((n, LANES), jnp.int32), jax.ShapeDtypeStruct((n, LANES), F32)],
        compiler_params=_cparams(("parallel",)),
        name="merge_router",
    )(*x, *o_ret, *g_r, *o_moba, ret_gn_w.reshape(1, -1), ret_gn_b.reshape(1, -1), w_out_b, ffn_norm_w.reshape(1, d),
      router_w, router_b.reshape(1, -1), _head_blockdiag(1.0 / HEAD_DIM))


ROW_BITS = 17
ROW_MASK = (1 << ROW_BITS) - 1


def _experts_kernel(ge_ref, pk_ref, hn_hbm, wg_ref, bg_ref, wu_ref, bu_ref, wd_ref, bd_ref, y_hbm,
                    xbuf0, xbuf1, ybuf0, ybuf1, wgb, wub, wdb, gsem, ssem):
    s = pl.program_id(0)
    last = pl.num_programs(0) - 1

    def packed(grp, r):
        return pk_ref[(grp + 1) * MOE_ROWS + r]

    def row_gather(grp, r, xb, sem):
        tok = lax.shift_right_logical(packed(grp, r), ROW_BITS)
        return pltpu.make_async_copy(hn_hbm.at[pl.ds(tok, 1), :], xb.at[pl.ds(r, 1), :], sem)

    def row_scatter(grp, r, yb, sem):
        dst = packed(grp, r) & ROW_MASK
        return pltpu.make_async_copy(yb.at[pl.ds(r, 1), :], y_hbm.at[pl.ds(dst, 1), :], sem)

    def wait_rows(xb, sem):
        pltpu.make_async_copy(hn_hbm.at[pl.ds(0, MOE_ROWS), :], xb, sem).wait()

    @pl.when(s == 0)
    def _():
        ybuf1[...] = jnp.zeros_like(ybuf1)
        for r in range(MOE_ROWS):
            row_gather(0, r, xbuf0, gsem.at[0]).start()

    @pl.when((s == 0) | (ge_ref[s] != ge_ref[jnp.maximum(s - 1, 0)]))
    def _():
        wgb[...] = wg_ref[...].astype(BF16)
        wub[...] = wu_ref[...].astype(BF16)
        wdb[...] = wd_ref[...].astype(BF16)

    def step(cur):
        xc, xo = (xbuf0, xbuf1) if cur == 0 else (xbuf1, xbuf0)
        yc, yo = (ybuf0, ybuf1) if cur == 0 else (ybuf1, ybuf0)
        oth = 1 - cur
        wait_rows(xc, gsem.at[cur])

        @pl.when(s >= 1)
        def _():
            wait_rows(yc, ssem.at[cur])

        for r in range(MOE_ROWS):
            row_gather(s + 1, r, xo, gsem.at[oth]).start()
            row_scatter(s - 1, r, yo, ssem.at[oth]).start()
        xb = xc[...].astype(BF16)
        gate = jnp.minimum(_dot(xb, wgb[...]) + bg_ref[...], SWIGLU_LIMIT)
        up = jnp.clip(_dot(xb, wub[...]) + bu_ref[...], -SWIGLU_LIMIT, SWIGLU_LIMIT)
        act = (up + 1.0) * gate * jax.nn.sigmoid(SWIGLU_ALPHA * gate)
        yc[...] = _dot(act.astype(BF16), wdb[...]) + bd_ref[...]

        @pl.when(s == last)
        def _():
            for r in range(MOE_ROWS):
                row_scatter(s, r, yc, ssem.at[cur]).start()
            wait_rows(yc, ssem.at[cur])
            wait_rows(yo, ssem.at[oth])
            wait_rows(xo, gsem.at[oth])

    @pl.when(s % 2 == 0)
    def _():
        step(0)

    @pl.when(s % 2 == 1)
    def _():
        step(1)


def _experts(hn, group_e, packed_rows, w_gate, b_gate, w_up, b_up, w_down, b_down):
    n, d = hn.shape
    steps = group_e.shape[0]
    d_ff = w_gate.shape[-1]
    wspec = lambda shape: pl.BlockSpec((None,) + shape, lambda g, ge, pk: (ge[g], 0, 0))
    rows = lambda dt: pltpu.VMEM((MOE_ROWS, d), dt)
    return pl.pallas_call(
        _experts_kernel,
        grid_spec=pltpu.PrefetchScalarGridSpec(
            num_scalar_prefetch=2,
            grid=(steps,),
            in_specs=[pl.BlockSpec(memory_space=pl.ANY), wspec((d, d_ff)), wspec((1, d_ff)), wspec((d, d_ff)),
                      wspec((1, d_ff)), wspec((d_ff, d)), wspec((1, d))],
            out_specs=pl.BlockSpec(memory_space=pl.ANY),
            scratch_shapes=[rows(F32), rows(F32), rows(F32), rows(F32),
                            pltpu.VMEM((d, d_ff), BF16), pltpu.VMEM((d, d_ff), BF16), pltpu.VMEM((d_ff, d), BF16),
                            pltpu.SemaphoreType.DMA((2,)), pltpu.SemaphoreType.DMA((2,))],
        ),
        out_shape=jax.ShapeDtypeStruct((TOP_K * n + 2 * MOE_ROWS, d), F32),
        compiler_params=_cparams(("arbitrary",)),
        name="experts",
    )(group_e, packed_rows, hn, w_gate, b_gate.reshape(N_EXPERTS, 1, d_ff), w_up, b_up.reshape(N_EXPERTS, 1, d_ff),
      w_down, b_down.reshape(N_EXPERTS, 1, d))


def _route(top_e):
    n = top_e.shape[0]
    n_assign = n * TOP_K
    assert (TOP_K * n + 2 * MOE_ROWS) <= (1 << ROW_BITS) and n <= (1 << (32 - ROW_BITS))
    flat_e = top_e.reshape(-1)
    order = jnp.argsort(flat_e).astype(jnp.int32)
    experts = jnp.arange(N_EXPERTS, dtype=jnp.int32)
    counts = jnp.sum((flat_e[:, None] == experts[None, :]).astype(jnp.int32), axis=0)
    starts = jnp.cumsum(counts) - counts
    padded = (counts + MOE_ROWS - 1) // MOE_ROWS * MOE_ROWS
    pad_ends = jnp.cumsum(padded)
    pad_starts = pad_ends - padded
    n_groups = -(-n_assign // MOE_ROWS) + N_EXPERTS
    g_start = jnp.arange(n_groups, dtype=jnp.int32) * MOE_ROWS
    group_e = jnp.minimum(jnp.sum((pad_ends[None, :] <= g_start[:, None]).astype(jnp.int32), axis=1), N_EXPERTS - 1)
    offset = (pad_starts - starts)[group_e]
    end = (starts + counts)[group_e]
    r = jnp.arange(MOE_ROWS, dtype=jnp.int32)[None, :]
    j = g_start[:, None] + r - offset[:, None]
    slot = order[jnp.clip(j, 0, n_assign - 1)]
    tok = slot // TOP_K
    parity = (jnp.arange(-1, n_groups + 2, dtype=jnp.int32) % 2)[:, None]
    spare = TOP_K * n + parity * MOE_ROWS + r
    real = j < end[:, None]
    word = jnp.where(real, (tok << ROW_BITS) | ((slot % TOP_K) * n + tok), spare[2:-1])
    words = jnp.concatenate([spare[:2], word, spare[-1:]], axis=0)
    group_e = jnp.concatenate([group_e[:1], group_e])
    return group_e.astype(jnp.int32), words.reshape(-1).astype(jnp.int32)


def _final_kernel(x1_ref, *rest, tiles_p):
    y_refs = rest[:TOP_K]
    gt_ref, pp_ref, ps_ref, nw_ref, wg_ref, wp_ref, op_ref, os_ref = rest[TOP_K:]
    i = pl.program_id(0)
    is_p = i < tiles_p
    gt = gt_ref[...]
    x2 = x1_ref[...]
    for k in range(TOP_K):
        x2 = x2 + y_refs[k][...] * gt[:, k:k + 1]
    hn = x2 * lax.rsqrt(jnp.mean(x2 * x2, axis=-1, keepdims=True) + EPS) * nw_ref[...]
    gate = jax.nn.sigmoid(_dot(hn.astype(BF16), wg_ref[...]))
    out = x2 + gate * _dot(_pick(is_p, pp_ref, ps_ref).astype(BF16), wp_ref[...])

    @pl.when(is_p)
    def _():
        op_ref[...] = out

    @pl.when(jnp.logical_not(is_p))
    def _():
        os_ref[...] = out


def _final(x1, y, gates, p, pl_norm_w, pl_gate_b, pl_proj_b):
    n, d = x1.shape
    n_p = p[0].shape[0]
    tm = TOKEN_TILE
    tiles_p = n_p // tm
    tiles = n // tm
    row = lambda i: (i, 0)
    full = lambda i: (0, 0)
    y_specs = [pl.BlockSpec((tm, d), functools.partial(lambda i, k: (k * tiles + i, 0), k=k)) for k in range(TOP_K)]
    return pl.pallas_call(
        functools.partial(_final_kernel, tiles_p=tiles_p),
        grid=(tiles,),
        in_specs=[pl.BlockSpec((tm, d), row), *y_specs,
                  pl.BlockSpec((tm, LANES), row), *_group_specs(p[0].shape[-1], tm, tiles_p),
                  pl.BlockSpec((1, d), full), pl.BlockSpec(pl_gate_b.shape, full), pl.BlockSpec(pl_proj_b.shape, full)],
        out_specs=list(_group_specs(d, tm, tiles_p)),
        out_shape=[jax.ShapeDtypeStruct((n_p, d), F32), jax.ShapeDtypeStruct((n - n_p, d), F32)],
        compiler_params=_cparams(("arbitrary",)),
        name="final",
    )(x1, *([y] * TOP_K), gates, *p, pl_norm_w.reshape(1, d), pl_gate_b, pl_proj_b)


def kernel(x_prompt, x_sample, cache_k, cache_v, page_table, state_ret, p_prompt, p_sample, attn_norm_w, w_in, q_norm_w, k_norm_w, ret_gn_w, ret_gn_b, w_out, ffn_norm_w, router_w, router_b, w_gate, b_gate, w_up, b_up, w_down, b_down, pl_norm_w, pl_gate_w, pl_proj_w):
    depth = w_in.shape[0]
    b, s, d = x_prompt.shape
    bd, sd, _ = x_sample.shape
    n_p = b * s
    n_s = bd * sd
    past_len = page_table.shape[1] * PAGE_SIZE
    pos_p = jnp.arange(s, dtype=jnp.int32)
    pos_s = past_len + jnp.arange(sd, dtype=jnp.int32)
    xp = x_prompt.reshape(n_p, d)
    xs = x_sample.reshape(n_s, d)
    heads_last = lambda t: jnp.transpose(t.reshape(b, N_HEADS, HEAD_DIM, s), (0, 3, 1, 2))
    kp_l, vp_l, sp_l, ks_l, vs_l, ss_l = [], [], [], [], [], []
    for i in range(depth):
        w_in_b = w_in[i].astype(BF16)
        qr, kr, vr, gr_p, qt, kt, vt, kb, vtb, kmean = _proj(
            xp, pos_p, attn_norm_w[i], w_in_b, q_norm_w[i], k_norm_w[i], prompt=True, seq_len=s)
        oret_p, st_p = _retention(qr, kr, vr, n_seq=b, L=RET_CHUNK, n_chunks=s // RET_CHUNK, seqs_per_step=1)
        omoba_p = _moba_prompt(qt, kb, vtb, kmean, n_seq=b, seq_len=s)
        qr, kr, vr, gr_s, qm_s, km_s, vm_s = _proj(
            xs, pos_s, attn_norm_w[i], w_in_b, q_norm_w[i], k_norm_w[i], prompt=False)
        oret_s, st_s = _retention(qr, kr, vr, n_seq=bd, L=sd, n_chunks=1, seqs_per_step=8, init_state=state_ret[i])
        omoba_s = _moba_sample(qm_s, km_s, vm_s, jnp.transpose(cache_k[i], (0, 2, 3, 1)),
                               jnp.transpose(cache_v[i], (0, 2, 3, 1)), page_table)
        x1, hn, top_e, gates = _merge(
            (xp, xs), (oret_p, oret_s), (gr_p, gr_s), (omoba_p, omoba_s), ret_gn_w[i], ret_gn_b[i],
            w_out[i].astype(BF16), ffn_norm_w[i], router_w[i], router_b[i])
        group_e, packed_rows = _route(top_e[:, :TOP_K])
        y = _experts(hn, group_e, packed_rows, w_gate[i], b_gate[i], w_up[i], b_up[i], w_down[i], b_down[i])
        xp, xs = _final(x1, y, gates,
                        (p_prompt[i].reshape(n_p, -1), p_sample[i].reshape(n_s, -1)),
                        pl_norm_w[i], pl_gate_w[i].astype(BF16), pl_proj_w[i].astype(BF16))
        kp_l.append(heads_last(kt))
        vp_l.append(heads_last(vt))
        sp_l.append(st_p)
        ks_l.append(km_s.reshape(bd, sd, N_HEADS, HEAD_DIM))
        vs_l.append(vm_s.reshape(bd, sd, N_HEADS, HEAD_DIM))
        ss_l.append(st_s)
    return (xp.reshape(b, s, d), xs.reshape(bd, sd, d), jnp.stack(kp_l), jnp.stack(vp_l), jnp.stack(sp_l),
            jnp.stack(ks_l), jnp.stack(vs_l), jnp.stack(ss_l))
```
